```python
import math
import jax, jax.numpy as jnp
from jax import lax
import numpy as np

D_MODEL = 1024
BATCH = 8
SEQ = 4096
DEPTH = 4

GRID_W = 64
CTX_LEN = 256
N_MIXERS = 2
N_SSD_LAYERS = (DEPTH + 1) // 2
N_ATTN_LAYERS = DEPTH // 2
EPS = 1e-6

SSD_EXPAND = 2
D_INNER = SSD_EXPAND * D_MODEL
SSD_HEAD_DIM = 64
SSD_HEADS = D_INNER // SSD_HEAD_DIM
SSD_GROUPS = 4
SSD_HEADS_PER_GROUP = SSD_HEADS // SSD_GROUPS
D_STATE = 128
CONV_W = 5
CHUNK = 128
D_CONV_CH = D_INNER + 2 * SSD_GROUPS * D_STATE
D_IN_PROJ = 2 * D_INNER + 2 * SSD_GROUPS * D_STATE + 2 * SSD_HEADS

DA_HEADS = D_MODEL // 128
DA_HEAD_DIM = 64
DA_V_DIM = 2 * DA_HEAD_DIM
DA_QKV = 3 * DA_HEADS * 2 * DA_HEAD_DIM
Q_BLOCK = 128
ROPE_THETA = 10000.0
ROPE_PAIRS_PER_AXIS = DA_HEAD_DIM // 4

D_FF = -(-8 * D_MODEL // (3 * 256)) * 256

kernel_name = "hybrid_ssd_diffattn_dit_prefix"


def rmsnorm(x, w):
    xf = x.astype(jnp.float32)
    y = xf * lax.rsqrt(jnp.mean(xf * xf, axis=-1, keepdims=True) + EPS)
    return (y * w.astype(jnp.float32)).astype(x.dtype)


def modulate(h, shift, scale):
    return h * (1 + scale) + shift


def swiglu(h, w_gate, w_up, w_down):
    return (jax.nn.silu(h @ w_gate) * (h @ w_up)) @ w_down


def dwconv_centred(u, w, b):
    y = lax.conv_general_dilated(
        u, w[:, None, :].astype(u.dtype), window_strides=(1,),
        padding=[(CONV_W // 2, CONV_W // 2)],
        dimension_numbers=("NWC", "WIO", "NWC"),
        feature_group_count=u.shape[-1])
    return y + b


def _segsum_exp(a_cum):
    n = a_cum.shape[-1]
    mask = jnp.tril(jnp.ones((n, n), dtype=bool))
    diff = a_cum[..., :, None] - a_cum[..., None, :]
    return jnp.exp(jnp.where(mask, diff, -jnp.inf))


def ssd_scan(x, dt, a, b, c, h0):
    bsz, L, H, P = x.shape
    G, N = b.shape[2], b.shape[3]
    Hg = H // G
    nc = L // CHUNK
    f32 = jnp.float32
    xd = (x.astype(f32) * dt[..., None]).reshape(bsz, nc, CHUNK, G, Hg, P)
    adt = (dt * a).reshape(bsz, nc, CHUNK, G, Hg).transpose(0, 3, 4, 1, 2)
    a_cum = jnp.cumsum(adt, axis=-1)
    bc = b.astype(f32).reshape(bsz, nc, CHUNK, G, N)
    cc = c.astype(f32).reshape(bsz, nc, CHUNK, G, N)
    cb = jnp.einsum('bclgn,bcsgn->bgcls', cc, bc)
    wmat = cb[:, :, None] * _segsum_exp(a_cum)
    y_diag = jnp.einsum('bghcls,bcsghp->bclghp', wmat, xd)
    decay_states = jnp.exp(a_cum[..., -1:] - a_cum).transpose(0, 3, 4, 1, 2)
    states = jnp.einsum('bclgn,bclghp->cbghpn', bc, xd * decay_states[..., None])
    chunk_decay = jnp.exp(a_cum[..., -1]).transpose(3, 0, 1, 2)

    def step(h, inp):
        s_c, d_c = inp
        return h * d_c[..., None, None] + s_c, h

    h_last, h_prev = lax.scan(step, h0.astype(f32), (states, chunk_decay))
    decay_out = jnp.exp(a_cum).transpose(0, 3, 4, 1, 2)
    y_off = jnp.einsum('bclgn,cbghpn->bclghp', cc, h_prev) * decay_out[..., None]
    y = (y_diag + y_off).reshape(bsz, L, H, P)
    return y, h_last


def ssd_mixer(h_lat, h_ctx, w_in, conv_w, conv_b, a_log, dt_bias, d_skip, gnorm_w, w_out, need_ctx_out):
    bsz = h_lat.shape[0]

    def project(h):
        L = h.shape[1]
        zxbcdt = h @ w_in
        z = zxbcdt[..., :D_INNER]
        xbc = zxbcdt[..., D_INNER:D_INNER + D_CONV_CH]
        dt = zxbcdt[..., D_INNER + D_CONV_CH:]
        xbc = jax.nn.silu(dwconv_centred(xbc, conv_w, conv_b))
        xs = xbc[..., :D_INNER].reshape(bsz, L, SSD_HEADS, SSD_HEAD_DIM)
        bm = xbc[..., D_INNER:D_INNER + SSD_GROUPS * D_STATE].reshape(bsz, L, SSD_GROUPS, D_STATE)
        cm = xbc[..., D_INNER + SSD_GROUPS * D_STATE:].reshape(bsz, L, SSD_GROUPS, D_STATE)
        dt = jax.nn.softplus(dt.astype(jnp.float32).reshape(bsz, L, 2, SSD_HEADS)
                             + dt_bias.astype(jnp.float32))
        return z, xs, bm, cm, dt

    def flip(t):
        return jnp.flip(t, axis=1)

    def finish(y, xs, z, ref):
        L = xs.shape[1]
        y = y + xs.astype(jnp.float32) * d_skip.astype(jnp.float32)[:, None]
        y = y.reshape(bsz, L, D_INNER)
        y = rmsnorm(y * jax.nn.silu(z.astype(jnp.float32)), gnorm_w)
        return (y @ w_out).astype(ref.dtype)

    a = -jnp.exp(a_log.astype(jnp.float32))
    h0 = jnp.zeros((bsz, SSD_GROUPS, SSD_HEADS_PER_GROUP, SSD_HEAD_DIM, D_STATE), jnp.float32)
    zc, xc, bcm, ccm, dtc = project(h_ctx)
    yc_f, sc_f = ssd_scan(xc, dtc[:, :, 0], a[0], bcm, ccm, h0)
    yc_b, sc_b = ssd_scan(flip(xc), flip(dtc[:, :, 1]), a[1], flip(bcm), flip(ccm), h0)
    zl, xl, bl, cl, dtl = project(h_lat)
    yl_f, _ = ssd_scan(xl, dtl[:, :, 0], a[0], bl, cl, sc_f)
    yl_b, _ = ssd_scan(flip(xl), flip(dtl[:, :, 1]), a[1], flip(bl), flip(cl), sc_b)
    out_lat = finish(yl_f + flip(yl_b), xl, zl, h_lat)
    out_ctx = finish(yc_f + flip(yc_b), xc, zc, h_ctx) if need_ctx_out else None
    return out_lat, out_ctx


def rope_2d(x, cos, sin):
    half = x.shape[-1] // 2
    c = cos[None, :, None, None, :].astype(x.dtype)
    s = sin[None, :, None, None, :].astype(x.dtype)
    x1, x2 = x[..., :half], x[..., half:]
    return jnp.concatenate([x1 * c - x2 * s, x2 * c + x1 * s], axis=-1)


def diff_attn(h_lat, h_ctx, w_qkv, w_o, lq1, lk1, lq2, lk2, subln_w, lambda_init, cos, sin, need_ctx_out):
    bsz = h_lat.shape[0]
    hq = DA_HEADS * 2 * DA_HEAD_DIM

    def project(h):
        L = h.shape[1]
        qkv = h @ w_qkv
        q = qkv[..., :hq].reshape(bsz, L, DA_HEADS, 2, DA_HEAD_DIM)
        k = qkv[..., hq:2 * hq].reshape(bsz, L, DA_HEADS, 2, DA_HEAD_DIM)
        v = qkv[..., 2 * hq:].reshape(bsz, L, DA_HEADS, DA_V_DIM)
        return q, k, v

    f32 = jnp.float32
    lam = (jnp.exp(jnp.sum(lq1.astype(f32) * lk1.astype(f32)))
           - jnp.exp(jnp.sum(lq2.astype(f32) * lk2.astype(f32))) + lambda_init)
    scale = DA_HEAD_DIM ** -0.5

    def attend(q, k, v):
        s = jnp.einsum('bqhmd,bkhmd->bhmqk', q, k, preferred_element_type=f32) * scale
        p = jax.nn.softmax(s, axis=-1)
        w = p[:, :, 0] - lam * p[:, :, 1]
        return jnp.einsum('bhqk,bkhe->bqhe', w.astype(v.dtype), v)

    def finish(o, ref):
        L = o.shape[1]
        o = rmsnorm(o, subln_w) * (1.0 - lambda_init)
        return (o.reshape(bsz, L, DA_HEADS * DA_V_DIM) @ w_o).astype(ref.dtype)

    qc, kc, vc = project(h_ctx)
    ql, kl, vl = project(h_lat)
    ql = rope_2d(ql, cos, sin)
    kl = rope_2d(kl, cos, sin)
    k_all = jnp.concatenate([kl, kc], axis=1)
    v_all = jnp.concatenate([vl, vc], axis=1)
    L = h_lat.shape[1]
    nb = L // Q_BLOCK
    q_blocks = ql.reshape(bsz, nb, Q_BLOCK, DA_HEADS, 2, DA_HEAD_DIM).transpose(1, 0, 2, 3, 4, 5)
    o_lat = lax.map(lambda qb: attend(qb, k_all, v_all), q_blocks)
    o_lat = o_lat.transpose(1, 0, 2, 3, 4).reshape(bsz, L, DA_HEADS, DA_V_DIM)
    out_lat = finish(o_lat, h_lat)
    out_ctx = finish(attend(qc, kc, vc), h_ctx) if need_ctx_out else None
    return out_lat, out_ctx


def axial_rope_tables(L):
    rows_n = L // GRID_W
    row = jnp.repeat(jnp.arange(rows_n, dtype=jnp.float32), GRID_W)
    col = jnp.tile(jnp.arange(GRID_W, dtype=jnp.float32), rows_n)
    freqs = 1.0 / (ROPE_THETA ** (jnp.arange(ROPE_PAIRS_PER_AXIS, dtype=jnp.float32) / ROPE_PAIRS_PER_AXIS))
    ang = jnp.concatenate([row[:, None] * freqs, col[:, None] * freqs], axis=-1)
    return jnp.cos(ang), jnp.sin(ang)


def setup_inputs(seed: int = 0) -> dict:
    key = jax.random.key(seed)
    ks = jax.random.split(key, 32)
    nrm = jax.random.normal
    f32 = jnp.float32
    D = D_MODEL
    dt0 = jnp.exp(jax.random.uniform(ks[10], (N_SSD_LAYERS, 2, SSD_HEADS), f32,
                                     math.log(1e-3), math.log(1e-1)))
    return {
        "x": nrm(ks[0], (BATCH, SEQ, D), f32),
        "c": nrm(ks[1], (BATCH, D), f32),
        "ctx": nrm(ks[2], (BATCH, CTX_LEN, D), f32),
        "c_ctx": nrm(ks[3], (D,), f32),
        "w_ada": nrm(ks[4], (DEPTH, D, 6 * D), f32) * (0.5 * D ** -0.5),
        "b_ada": nrm(ks[5], (DEPTH, 6 * D), f32) * 0.02,
        "norm1_w": 1.0 + 0.05 * nrm(ks[6], (DEPTH, D), f32),
        "norm2_w": 1.0 + 0.05 * nrm(ks[7], (DEPTH, D), f32),
        "ssd_w_in": nrm(ks[8], (N_SSD_LAYERS, D, D_IN_PROJ), f32) * D ** -0.5,
        "ssd_conv_w": nrm(ks[9], (N_SSD_LAYERS, CONV_W, D_CONV_CH), f32) * CONV_W ** -0.5,
        "ssd_conv_b": nrm(ks[11], (N_SSD_LAYERS, D_CONV_CH), f32) * 0.02,
        "ssd_a_log": jnp.log(jax.random.uniform(ks[12], (N_SSD_LAYERS, 2, SSD_HEADS), f32, 1.0, 16.0)),
        "ssd_dt_bias": dt0 + jnp.log(-jnp.expm1(-dt0)),
        "ssd_d": 1.0 + 0.05 * nrm(ks[13], (N_SSD_LAYERS, SSD_HEADS), f32),
        "ssd_norm_w": 1.0 + 0.05 * nrm(ks[14], (N_SSD_LAYERS, D_INNER), f32),
        "ssd_w_out": nrm(ks[15], (N_SSD_LAYERS, D_INNER, D), f32) * D_INNER ** -0.5,
        "da_w_qkv": nrm(ks[16], (N_ATTN_LAYERS, D, DA_QKV), f32) * D ** -0.5,
        "da_w_o": nrm(ks[17], (N_ATTN_LAYERS, DA_HEADS * DA_V_DIM, D), f32) * (DA_HEADS * DA_V_DIM) ** -0.5,
        "da_lq1": nrm(ks[18], (N_ATTN_LAYERS, DA_HEAD_DIM), f32) * 0.1,
        "da_lk1": nrm(ks[19], (N_ATTN_LAYERS, DA_HEAD_DIM), f32) * 0.1,
        "da_lq2": nrm(ks[20], (N_ATTN_LAYERS, DA_HEAD_DIM), f32) * 0.1,
        "da_lk2": nrm(ks[21], (N_ATTN_LAYERS, DA_HEAD_DIM), f32) * 0.1,
        "da_subln_w": 1.0 + 0.05 * nrm(ks[22], (N_ATTN_LAYERS, DA_V_DIM), f32),
        "ffn_w_gate": nrm(ks[23], (DEPTH, D, D_FF), f32) * D ** -0.5,
        "ffn_w_up": nrm(ks[24], (DEPTH, D, D_FF), f32) * D ** -0.5,
        "ffn_w_down": nrm(ks[25], (DEPTH, D_FF, D), f32) * D_FF ** -0.5,
        "final_norm_w": 1.0 + 0.05 * nrm(ks[26], (D,), f32),
    }


def reference(x, c, ctx, c_ctx, w_ada, b_ada, norm1_w, norm2_w, ssd_w_in, ssd_conv_w, ssd_conv_b,
              ssd_a_log, ssd_dt_bias, ssd_d, ssd_norm_w, ssd_w_out, da_w_qkv, da_w_o, da_lq1, da_lk1,
              da_lq2, da_lk2, da_subln_w, ffn_w_gate, ffn_w_up, ffn_w_down, final_norm_w):
    cos, sin = axial_rope_tables(x.shape[1])
    s_lat = jax.nn.silu(c)[:, None, :]
    s_ctx = jax.nn.silu(c_ctx)[None, None, :]
    x_lat, x_ctx = x, ctx
    for i in range(DEPTH):
        need_ctx = i < DEPTH - 1
        sh1, sc1, g1, sh2, sc2, g2 = jnp.split(s_lat @ w_ada[i] + b_ada[i], 6, axis=-1)
        csh1, csc1, cg1, csh2, csc2, cg2 = jnp.split(s_ctx @ w_ada[i] + b_ada[i], 6, axis=-1)
        a_lat = modulate(rmsnorm(x_lat, norm1_w[i]), sh1, sc1)
        a_ctx = modulate(rmsnorm(x_ctx, norm1_w[i]), csh1, csc1)
        j = i // N_MIXERS
        if i % N_MIXERS == 0:
            m_lat, m_ctx = ssd_mixer(a_lat, a_ctx, ssd_w_in[j], ssd_conv_w[j], ssd_conv_b[j],
                                     ssd_a_log[j], ssd_dt_bias[j], ssd_d[j], ssd_norm_w[j],
                                     ssd_w_out[j], need_ctx)
        else:
            lambda_init = 0.8 - 0.6 * math.exp(-0.3 * i)
            m_lat, m_ctx = diff_attn(a_lat, a_ctx, da_w_qkv[j], da_w_o[j], da_lq1[j], da_lk1[j],
                                     da_lq2[j], da_lk2[j], da_subln_w[j], lambda_init, cos, sin,
                                     need_ctx)
        x_lat = x_lat + g1 * m_lat
        f_lat = swiglu(modulate(rmsnorm(x_lat, norm2_w[i]), sh2, sc2),
                       ffn_w_gate[i], ffn_w_up[i], ffn_w_down[i])
        x_lat = x_lat + g2 * f_lat
        if need_ctx:
            x_ctx = x_ctx + cg1 * m_ctx
            f_ctx = swiglu(modulate(rmsnorm(x_ctx, norm2_w[i]), csh2, csc2),
                           ffn_w_gate[i], ffn_w_up[i], ffn_w_down[i])
            x_ctx = x_ctx + cg2 * f_ctx
    return rmsnorm(x_lat, final_norm_w)
```

```python
import functools
import math

import jax
import jax.numpy as jnp
from jax import lax
from jax.experimental import pallas as pl
from jax.experimental.pallas import tpu as pltpu

F32 = jnp.float32
BF16 = jnp.bfloat16

D_MODEL = 1024
DEPTH = 4
GRID_W = 64
EPS = 1e-6

D_INNER = 2 * D_MODEL
SSD_HEAD_DIM = 64
SSD_HEADS = D_INNER // SSD_HEAD_DIM
SSD_GROUPS = 4
HEADS_PER_GROUP = SSD_HEADS // SSD_GROUPS
D_STATE = 128
CONV_W = 5
CHUNK = 128
D_BC = SSD_GROUPS * D_STATE
D_CONV_CH = D_INNER + 2 * D_BC
D_ZX = D_INNER + D_CONV_CH

DA_HEADS = D_MODEL // 128
DA_HEAD_DIM = 64
ROPE_THETA = 10000.0
ROPE_PAIRS = DA_HEAD_DIM // 4

D_FF = -(-8 * D_MODEL // (3 * 256)) * 256

LANES = 128
MOD_ROWS = 16
VMEM_LIMIT = 56 * 1024 * 1024


def _cparams(sem):
    return pltpu.CompilerParams(dimension_semantics=sem, vmem_limit_bytes=VMEM_LIMIT)


def _dot(a, b):
    return jnp.dot(a, b, preferred_element_type=F32)


def _dot_nt(a, b):
    return lax.dot_general(a, b, (((1,), (1,)), ((), ())), preferred_element_type=F32)


def _sigmoid(x):
    return 1.0 / (1.0 + jnp.exp(-x))


def _rms(xf, w):
    return xf * lax.rsqrt(jnp.mean(xf * xf, axis=-1, keepdims=True) + EPS) * w


def _resident(shape, index_map):
    return pl.BlockSpec(shape, index_map, pipeline_mode=pl.Buffered(1))


def _ada_kernel(c_ref, w_ref, b_ref, o_ref):
    cv = c_ref[...]
    s = (cv * _sigmoid(cv)).astype(BF16)
    o_ref[...] = _dot(s, w_ref[...]) + b_ref[...]


def _ada_call(cvec, w_ada, b_ada):
    depth, d, n = w_ada.shape
    tn = 2048
    return pl.pallas_call(
        _ada_kernel,
        grid=(depth, n // tn),
        in_specs=[
            pl.BlockSpec((MOD_ROWS, d), lambda l, j: (0, 0)),
            pl.BlockSpec((None, d, tn), lambda l, j: (l, 0, j)),
            pl.BlockSpec((None, 1, tn), lambda l, j: (l, 0, j)),
        ],
        out_specs=pl.BlockSpec((None, MOD_ROWS, tn), lambda l, j: (l, 0, j)),
        out_shape=jax.ShapeDtypeStruct((depth, MOD_ROWS, n), F32),
        compiler_params=_cparams(("arbitrary", "arbitrary")),
        name="adaln",
    )(cvec, w_ada, b_ada.reshape(depth, 1, n))


class _Geom:
    def __init__(self, bsz, seq, ctx):
        self.bsz, self.seq, self.ctx = bsz, seq, ctx
        self.n_lat = bsz * seq
        self.n_ctx = bsz * ctx
        self.rows = self.n_lat + self.n_ctx
        tm = 1024
        while seq % tm or self.n_ctx % tm:
            tm //= 2
        self.tm = tm

    def mod_spec(self, layer, k, tm):
        n_lat_tiles = self.n_lat // tm
        per_batch = self.seq // tm
        bsz = self.bsz

        def imap(i, *_):
            row = jnp.where(i < n_lat_tiles, i // per_batch, bsz)
            return ((layer * MOD_ROWS + row) * 6 + k, 0, 0)

        return pl.BlockSpec((None, 1, D_MODEL), imap)


def _softplus(x):
    return jnp.maximum(x, 0.0) + jnp.log1p(jnp.exp(-jnp.abs(x)))


def _ssd_in_kernel(x_ref, nw_ref, sh_ref, sc_ref, w_ref, wdt_ref, dtb_ref, zx_ref, dt_ref, a_scr):
    @pl.when(pl.program_id(1) == 0)
    def _():
        a = _rms(x_ref[...], nw_ref[...]) * (1.0 + sc_ref[...]) + sh_ref[...]
        ab = a.astype(BF16)
        a_scr[...] = ab
        for d in range(2):
            dt_ref[d] = _softplus(_dot(ab, wdt_ref[d]) + dtb_ref[d])

    zx_ref[...] = _dot(a_scr[...], w_ref[...]).astype(BF16)


def _ssd_in_call(geom, layer, x_all, mod3, norm_w, w_zx, w_dt, dt_bias):
    tm, tn = geom.tm, 1024
    rows = geom.rows
    return pl.pallas_call(
        _ssd_in_kernel,
        grid=(rows // tm, D_ZX // tn),
        in_specs=[
            pl.BlockSpec((tm, D_MODEL), lambda i, j: (i, 0)),
            pl.BlockSpec((1, D_MODEL), lambda i, j: (0, 0)),
            geom.mod_spec(layer, 0, tm),
            geom.mod_spec(layer, 1, tm),
            pl.BlockSpec((D_MODEL, tn), lambda i, j: (0, j)),
            pl.BlockSpec((2, D_MODEL, LANES), lambda i, j: (0, 0, 0)),
            pl.BlockSpec((2, 1, LANES), lambda i, j: (0, 0, 0)),
        ],
        out_specs=[
            pl.BlockSpec((tm, tn), lambda i, j: (i, j)),
            pl.BlockSpec((2, tm, LANES), lambda i, j: (0, i, 0)),
        ],
        out_shape=[
            jax.ShapeDtypeStruct((rows, D_ZX), BF16),
            jax.ShapeDtypeStruct((2, rows, LANES), F32),
        ],
        scratch_shapes=[pltpu.VMEM((tm, D_MODEL), BF16)],
        compiler_params=_cparams(("arbitrary", "arbitrary")),
        name="ssd_in_proj",
    )(x_all, norm_w, mod3, mod3, w_zx, w_dt, dt_bias)


CONV_ROWS = 256
HALO = 16


def _conv_kernel(cur_ref, prev_ref, next_ref, w_ref, b_ref, o_ref, *, tiles_lat, per_lat, per_ctx):
    i = pl.program_id(0)
    in_lat = i < tiles_lat
    pos = jnp.where(in_lat, i % per_lat, (i - tiles_lat) % per_ctx)
    last = jnp.where(in_lat, per_lat - 1, per_ctx - 1)
    keep_prev = (pos != 0).astype(F32)
    keep_next = (pos != last).astype(F32)
    cur = cur_ref[...].astype(F32)
    prev = prev_ref[...].astype(F32)[HALO - 8:, :] * keep_prev
    nxt = next_ref[...].astype(F32)[:8, :] * keep_next
    ext = jnp.concatenate([prev, cur, nxt], axis=0)
    tr = cur.shape[0]
    w = w_ref[...]
    acc = b_ref[...] + w[0:1, :] * ext[6:6 + tr, :]
    for k in range(1, CONV_W):
        acc = acc + w[k:k + 1, :] * ext[6 + k:6 + k + tr, :]
    o_ref[...] = (acc * _sigmoid(acc)).astype(BF16)


def _conv_call(geom, zx, conv_w, conv_b):
    tr = CONV_ROWS
    ct = 1024
    rows = geom.rows
    col0 = D_INNER // ct
    hb = tr // HALO
    n_halo_blocks = rows // HALO
    kern = functools.partial(_conv_kernel, tiles_lat=geom.n_lat // tr, per_lat=geom.seq // tr,
                             per_ctx=geom.ctx // tr)
    return pl.pallas_call(
        kern,
        grid=(rows // tr, D_CONV_CH // ct),
        in_specs=[
            pl.BlockSpec((tr, ct), lambda i, j: (i, col0 + j)),
            pl.BlockSpec((HALO, ct), lambda i, j: (jnp.maximum(i * hb - 1, 0), col0 + j)),
            pl.BlockSpec((HALO, ct), lambda i, j: (jnp.minimum((i + 1) * hb, n_halo_blocks - 1), col0 + j)),
            pl.BlockSpec((CONV_W, ct), lambda i, j: (0, j)),
            pl.BlockSpec((1, ct), lambda i, j: (0, j)),
        ],
        out_specs=pl.BlockSpec((tr, ct), lambda i, j: (i, j)),
        out_shape=jax.ShapeDtypeStruct((rows, D_CONV_CH), BF16),
        compiler_params=_cparams(("arbitrary", "arbitrary")),
        name="ssd_conv",
    )(zx, zx, zx, conv_w, conv_b)


def _split3(v):
    p1 = v.astype(BF16)
    r1 = v - p1.astype(F32)
    p2 = r1.astype(BF16)
    p3 = (r1 - p2.astype(F32)).astype(BF16)
    return p1, p2, p3


def _expand_heads(v, e_ref):
    p1, p2, p3 = _split3(v)
    e = e_ref[...]
    return _dot(p1, e) + _dot(p2, e) + _dot(p3, e)


def _ssd_scan_kernel(xs_ref, b_ref, c_ref, dt_ref, alog_ref, e_ref, y_ref, h_ref):
    d = pl.program_id(0)
    s = pl.program_id(2)

    @pl.when(s == 0)
    def _():
        h_ref[...] = jnp.zeros_like(h_ref)

    nh = SSD_HEADS
    dt = dt_ref[:, :nh]
    a = -jnp.exp(alog_ref[...])
    adt = dt * a
    li = lax.broadcasted_iota(jnp.int32, (CHUNK, CHUNK), 0)
    si = lax.broadcasted_iota(jnp.int32, (CHUNK, CHUNK), 1)
    sign = jnp.where(d == 0, 1, -1)
    mask = (si - li) * sign <= 0
    cum = jnp.dot(mask.astype(F32), adt, preferred_element_type=F32,
                  precision=lax.Precision.HIGHEST)
    tot = jnp.sum(adt, axis=0, keepdims=True)
    pad = jnp.zeros((CHUNK, CHUNK - 2 * nh), F32)
    tr = jnp.concatenate([cum, dt, pad], axis=1).T
    decay_out = jnp.exp(cum)
    w_state = dt * jnp.exp(tot - cum)
    cdec = jnp.broadcast_to(jnp.exp(tot), (8, nh))
    ex = _expand_heads(jnp.concatenate([w_state, decay_out, cdec], axis=0), e_ref)
    wst_e = ex[:CHUNK]
    dout_e = ex[CHUNK:2 * CHUNK]
    cdec_e = ex[2 * CHUNK:2 * CHUNK + 1]

    lane = lax.broadcasted_iota(jnp.int32, (CHUNK, 2 * SSD_HEAD_DIM), 1)
    gw = HEADS_PER_GROUP * SSD_HEAD_DIM
    for g in range(SSD_GROUPS):
        bg = b_ref[:, g * D_STATE:(g + 1) * D_STATE]
        cg = c_ref[:, g * D_STATE:(g + 1) * D_STATE]
        cb = _dot_nt(cg, bg)
        h_old = h_ref[:, g * gw:(g + 1) * gw]
        y_off = _dot(cg, h_old.astype(BF16)) * dout_e[:, g * gw:(g + 1) * gw]
        xg = xs_ref[:, g * gw:(g + 1) * gw]
        for q in range(HEADS_PER_GROUP // 2):
            ws = []
            for hh in range(2):
                h = g * HEADS_PER_GROUP + 2 * q + hh
                diff = cum[:, h:h + 1] - tr[h:h + 1, :]
                lmat = jnp.exp(jnp.where(mask, diff, -jnp.inf))
                ws.append((cb * lmat * tr[nh + h:nh + h + 1, :]).astype(BF16))
            xp = xg[:, q * 2 * SSD_HEAD_DIM:(q + 1) * 2 * SSD_HEAD_DIM]
            zero = jnp.zeros_like(xp)
            rhs = jnp.concatenate([jnp.where(lane < SSD_HEAD_DIM, xp, zero),
                                   jnp.where(lane >= SSD_HEAD_DIM, xp, zero)], axis=0)
            yd = _dot(jnp.concatenate(ws, axis=1), rhs)
            c0 = g * gw + q * 2 * SSD_HEAD_DIM
            y_ref[:, c0:c0 + 2 * SSD_HEAD_DIM] = yd + y_off[:, q * 2 * SSD_HEAD_DIM:(q + 1) * 2 * SSD_HEAD_DIM]
        xw = (xg.astype(F32) * wst_e[:, g * gw:(g + 1) * gw]).astype(BF16)
        bgt = bg.astype(F32).T.astype(BF16)
        h_ref[:, g * gw:(g + 1) * gw] = h_old * cdec_e[:, g * gw:(g + 1) * gw] + _dot(bgt, xw)


def _ssd_scan_call(geom, xbc, dt, a_log, e_mat):
    rows = geom.rows
    nc_lat = geom.seq // CHUNK
    nc_ctx = geom.ctx // CHUNK
    lat_blocks = geom.n_lat // CHUNK

    def chunk_block(d, b, s):
        in_ctx = s < nc_ctx
        c_ctx = jnp.where(d == 0, s, nc_ctx - 1 - s)
        sl = s - nc_ctx
        c_lat = jnp.where(d == 0, sl, nc_lat - 1 - sl)
        return jnp.where(in_ctx, lat_blocks + b * nc_ctx + c_ctx, b * nc_lat + c_lat)

    xcols = D_INNER // D_BC
    return pl.pallas_call(
        _ssd_scan_kernel,
        grid=(2, geom.bsz, nc_ctx + nc_lat),
        in_specs=[
            pl.BlockSpec((CHUNK, D_INNER), lambda d, b, s: (chunk_block(d, b, s), 0)),
            pl.BlockSpec((CHUNK, D_BC), lambda d, b, s: (chunk_block(d, b, s), xcols)),
            pl.BlockSpec((CHUNK, D_BC), lambda d, b, s: (chunk_block(d, b, s), xcols + 1)),
            pl.BlockSpec((None, CHUNK, LANES), lambda d, b, s: (d, chunk_block(d, b, s), 0)),
            pl.BlockSpec((None, 1, SSD_HEADS), lambda d, b, s: (d, 0, 0)),
            pl.BlockSpec((SSD_HEADS, D_INNER), lambda d, b, s: (0, 0)),
        ],
        out_specs=pl.BlockSpec((None, CHUNK, D_INNER), lambda d, b, s: (d, chunk_block(d, b, s), 0)),
        out_shape=jax.ShapeDtypeStruct((2, rows, D_INNER), F32),
        scratch_shapes=[pltpu.VMEM((D_STATE, D_INNER), F32)],
        compiler_params=_cparams(("arbitrary", "arbitrary", "arbitrary")),
        name="ssd_scan",
    )(xbc, xbc, xbc, dt, a_log, e_mat)


def _ssd_out_kernel(y_ref, xs_ref, z_ref, dsk_ref, gw_ref, w_ref, x_ref, gate_ref, o_ref):
    y = y_ref[0] + y_ref[1] + xs_ref[...].astype(F32) * dsk_ref[...]
    z = z_ref[...].astype(F32)
    yn = _rms(y * (z * _sigmoid(z)), gw_ref[...]).astype(BF16)
    o_ref[...] = x_ref[...] + gate_ref[...] * _dot(yn, w_ref[...])


def _ssd_out_call(geom, layer, y, xbc, zx, d_skip_e, gnorm_w, w_out, x_all, mod3):
    tm = min(geom.tm, 512)
    rows = geom.rows
    return pl.pallas_call(
        _ssd_out_kernel,
        grid=(rows // tm,),
        in_specs=[
            pl.BlockSpec((2, tm, D_INNER), lambda i: (0, i, 0)),
            pl.BlockSpec((tm, D_INNER), lambda i: (i, 0)),
            pl.BlockSpec((tm, D_INNER), lambda i: (i, 0)),
            pl.BlockSpec((1, D_INNER), lambda i: (0, 0)),
            pl.BlockSpec((1, D_INNER), lambda i: (0, 0)),
            _resident((D_INNER, D_MODEL), lambda i: (0, 0)),
            pl.BlockSpec((tm, D_MODEL), lambda i: (i, 0)),
            geom.mod_spec(layer, 2, tm),
        ],
        out_specs=pl.BlockSpec((tm, D_MODEL), lambda i: (i, 0)),
        out_shape=jax.ShapeDtypeStruct((rows, D_MODEL), F32),
        compiler_params=_cparams(("arbitrary",)),
        name="ssd_out_proj",
    )(y, xbc, zx, d_skip_e, gnorm_w, w_out, x_all, mod3)


def _rope_blocks(acc, cos, sin):
    outs = []
    for c in range(acc.shape[1] // LANES):
        xb = acc[:, c * LANES:(c + 1) * LANES]
        outs.append(xb * cos + pltpu.roll(xb, LANES // 2, 1) * sin)
    return outs


def _qkv_kernel(x_ref, nw_ref, sh_ref, sc_ref, w_ref, cos_ref, sin_ref,
                q1_ref, q2_ref, k_ref, v_ref, a_scr):
    j = pl.program_id(1)

    @pl.when(j == 0)
    def _():
        a = _rms(x_ref[...], nw_ref[...]) * (1.0 + sc_ref[...]) + sh_ref[...]
        a_scr[...] = a.astype(BF16)

    acc = _dot(a_scr[...], w_ref[...])

    @pl.when(j == 0)
    def _():
        scale = DA_HEAD_DIM ** -0.5
        lane = lax.broadcasted_iota(jnp.int32, (acc.shape[0], LANES), 1)
        is_map1 = (lane % (LANES // 2)) < (LANES // 4)
        for c, blk in enumerate(_rope_blocks(acc, cos_ref[...], sin_ref[...])):
            blk = blk * scale
            q1_ref[:, c * LANES:(c + 1) * LANES] = jnp.where(is_map1, blk, 0.0).astype(BF16)
            q2_ref[:, c * LANES:(c + 1) * LANES] = jnp.where(is_map1, 0.0, blk).astype(BF16)

    @pl.when(j == 1)
    def _():
        for c, blk in enumerate(_rope_blocks(acc, cos_ref[...], sin_ref[...])):
            k_ref[:, c * LANES:(c + 1) * LANES] = blk.astype(BF16)

    @pl.when(j == 2)
    def _():
        v_ref[...] = acc.astype(BF16)


def _qkv_call(geom, layer, x_all, mod3, norm_w, w_qkv, cos_t, sin_t):
    tm = geom.tm
    rows = geom.rows
    row_out = pl.BlockSpec((tm, D_MODEL), lambda i, j: (i, 0))
    return pl.pallas_call(
        _qkv_kernel,
        grid=(rows // tm, 3),
        in_specs=[
            pl.BlockSpec((tm, D_MODEL), lambda i, j: (i, 0)),
            pl.BlockSpec((1, D_MODEL), lambda i, j: (0, 0)),
            geom.mod_spec(layer, 0, tm),
            geom.mod_spec(layer, 1, tm),
            pl.BlockSpec((D_MODEL, D_MODEL), lambda i, j: (0, j)),
            pl.BlockSpec((tm, LANES), lambda i, j: (i, 0)),
            pl.BlockSpec((tm, LANES), lambda i, j: (i, 0)),
        ],
        out_specs=[row_out, row_out, row_out, row_out],
        out_shape=[jax.ShapeDtypeStruct((rows, D_MODEL), BF16)] * 4,
        scratch_shapes=[pltpu.VMEM((tm, D_MODEL), BF16)],
        compiler_params=_cparams(("arbitrary", "arbitrary")),
        name="attn_qkv_rope",
    )(x_all, norm_w, mod3, mod3, w_qkv, cos_t, sin_t)


def _flash_kernel(lam_ref, subw_ref, q1_ref, q2_ref, *refs, seg_lens, tk, lambda_init):
    n_seg = len(seg_lens)
    k_refs = refs[0:2 * n_seg:2]
    v_refs = refs[1:2 * n_seg:2]
    o_ref = refs[2 * n_seg + 1]
    acc_refs = refs[2 * n_seg + 2:2 * n_seg + 4]
    m_refs = refs[2 * n_seg + 4:2 * n_seg + 6]
    tq = q1_ref.shape[0]
    qs = (q1_ref[...], q2_ref[...])
    for t in range(2):
        acc_refs[t][...] = jnp.zeros_like(acc_refs[t])
        m_refs[t][...] = jnp.full_like(m_refs[t], -jnp.inf)

    def block(k_blk, v_blk):
        vaug = jnp.concatenate([v_blk, jnp.ones_like(v_blk)], axis=1)
        for t in range(2):
            s = _dot_nt(qs[t], k_blk)
            m_old = m_refs[t][...]
            m_new = jnp.maximum(m_old, jnp.max(s, axis=1, keepdims=True))
            alpha = jnp.exp(m_old - m_new)
            p = jnp.exp(s - m_new[:, :1]).astype(BF16)
            m_refs[t][...] = m_new
            acc_refs[t][...] = (jnp.concatenate([alpha, alpha], axis=1) * acc_refs[t][...]
                                + _dot(p, vaug))

    for k_ref, v_ref, slen in zip(k_refs, v_refs, seg_lens):
        bk = min(tk, slen)
        nblk = slen // bk
        if nblk == 1:
            block(k_ref[...], v_ref[...])
        else:
            def body(jb, carry, k_ref=k_ref, v_ref=v_ref, bk=bk):
                off = pl.multiple_of(jb * bk, bk)
                block(k_ref[pl.ds(off, bk), :], v_ref[pl.ds(off, bk), :])
                return carry
            lax.fori_loop(0, nblk, body, 0)

    lam_v = lam_ref[...]
    hd = DA_HEAD_DIM
    lam = (jnp.exp(jnp.sum(lam_v[:, 0:hd] * lam_v[:, hd:2 * hd], axis=1, keepdims=True))
           - jnp.exp(jnp.sum(lam_v[:, 2 * hd:3 * hd] * lam_v[:, 3 * hd:4 * hd], axis=1, keepdims=True))
           + lambda_init)
    a1 = acc_refs[0][...]
    a2 = acc_refs[1][...]
    o = a1[:, :LANES] / a1[:, LANES:] - lam * (a2[:, :LANES] / a2[:, LANES:])
    o = _rms(o, subw_ref[...]) * (1.0 - lambda_init)
    o_ref[...] = o.astype(BF16)


def _flash_call(geom, q1, q2, k, v, lam_vec, subln_w, lambda_init, o_prev, ctx_queries):
    bsz, seq, ctx = geom.bsz, geom.seq, geom.ctx
    tk = 512
    if ctx_queries:
        tq = min(256, ctx)
        nq = ctx // tq
        q_base = geom.n_lat // tq
        seg_lens = (ctx,)
    else:
        tq = min(256, seq)
        nq = seq // tq
        q_base = 0
        seg_lens = (seq, ctx)
    ctx_base = geom.n_lat // ctx

    def q_map(b, h, i):
        return (q_base + b * nq + i, h)

    kv_lat = pl.BlockSpec((seq, LANES), lambda b, h, i: (b, h))
    kv_ctx = pl.BlockSpec((ctx, LANES), lambda b, h, i: (ctx_base + b, h))
    kv_specs = [kv_ctx, kv_ctx] if ctx_queries else [kv_lat, kv_lat, kv_ctx, kv_ctx]
    kv_args = [k, v] if ctx_queries else [k, v, k, v]
    kern = functools.partial(_flash_kernel, seg_lens=seg_lens, tk=tk, lambda_init=lambda_init)
    n_in = 4 + len(kv_args)
    return pl.pallas_call(
        kern,
        grid=(bsz, DA_HEADS, nq),
        in_specs=[
            pl.BlockSpec((1, 4 * DA_HEAD_DIM), lambda b, h, i: (0, 0)),
            pl.BlockSpec((1, LANES), lambda b, h, i: (0, 0)),
            pl.BlockSpec((tq, LANES), q_map),
            pl.BlockSpec((tq, LANES), q_map),
            *kv_specs,
            pl.BlockSpec(memory_space=pl.ANY),
        ],
        out_specs=pl.BlockSpec((tq, LANES), q_map),
        out_shape=jax.ShapeDtypeStruct((geom.rows, D_MODEL), BF16),
        input_output_aliases={n_in: 0},
        scratch_shapes=[pltpu.VMEM((tq, 2 * LANES), F32), pltpu.VMEM((tq, 2 * LANES), F32),
                        pltpu.VMEM((tq, LANES), F32), pltpu.VMEM((tq, LANES), F32)],
        compiler_params=_cparams(("arbitrary", "arbitrary", "arbitrary")),
        name="flash_ctx" if ctx_queries else "flash_lat",
    )(lam_vec, subln_w, q1, q2, *kv_args, o_prev)


def _attn_out_kernel(o_ref, w_ref, x_ref, gate_ref, out_ref):
    out_ref[...] = x_ref[...] + gate_ref[...] * _dot(o_ref[...], w_ref[...])


def _attn_out_call(geom, layer, o, w_o, x_all, mod3):
    tm = geom.tm
    rows = geom.rows
    return pl.pallas_call(
        _attn_out_kernel,
        grid=(rows // tm,),
        in_specs=[
            pl.BlockSpec((tm, D_MODEL), lambda i: (i, 0)),
            _resident((D_MODEL, D_MODEL), lambda i: (0, 0)),
            pl.BlockSpec((tm, D_MODEL), lambda i: (i, 0)),
            geom.mod_spec(layer, 2, tm),
        ],
        out_specs=pl.BlockSpec((tm, D_MODEL), lambda i: (i, 0)),
        out_shape=jax.ShapeDtypeStruct((rows, D_MODEL), F32),
        compiler_params=_cparams(("arbitrary",)),
        name="attn_out_proj",
    )(o, w_o, x_all, mod3)


FFN_TILE = 256


def _ffn_kernel(x_ref, nw_ref, sh_ref, sc_ref, gate_ref, wg_ref, wu_ref, wd_ref, fw_ref,
                o_ref, h_scr, *, final_norm):
    x = x_ref[...]
    a = (_rms(x, nw_ref[...]) * (1.0 + sc_ref[...]) + sh_ref[...]).astype(BF16)
    for f in range(0, D_FF, FFN_TILE):
        g = _dot(a, wg_ref[:, f:f + FFN_TILE])
        u = _dot(a, wu_ref[:, f:f + FFN_TILE])
        h_scr[:, f:f + FFN_TILE] = (g * _sigmoid(g) * u).astype(BF16)
    out = x + gate_ref[...] * _dot(h_scr[...], wd_ref[...])
    if final_norm:
        out = _rms(out, fw_ref[...])
    o_ref[...] = out


def _ffn_call(geom, layer, x_all, mod3, norm_w, w_gate, w_up, w_down, final_w, final_norm):
    tm = min(geom.tm, 512)
    rows = geom.n_lat if final_norm else geom.rows
    kern = functools.partial(_ffn_kernel, final_norm=final_norm)
    return pl.pallas_call(
        kern,
        grid=(rows // tm,),
        in_specs=[
            pl.BlockSpec((tm, D_MODEL), lambda i: (i, 0)),
            pl.BlockSpec((1, D_MODEL), lambda i: (0, 0)),
            geom.mod_spec(layer, 3, tm),
            geom.mod_spec(layer, 4, tm),
            geom.mod_spec(layer, 5, tm),
            _resident((D_MODEL, D_FF), lambda i: (0, 0)),
            _resident((D_MODEL, D_FF), lambda i: (0, 0)),
            _resident((D_FF, D_MODEL), lambda i: (0, 0)),
            pl.BlockSpec((1, D_MODEL), lambda i: (0, 0)),
        ],
        out_specs=pl.BlockSpec((tm, D_MODEL), lambda i: (i, 0)),
        out_shape=jax.ShapeDtypeStruct((rows, D_MODEL), F32),
        scratch_shapes=[pltpu.VMEM((tm, D_FF), BF16)],
        compiler_params=_cparams(("arbitrary",)),
        name="ffn_swiglu",
    )(x_all, norm_w, mod3, mod3, mod3, w_gate, w_up, w_down, final_w)


def _rope_tables(geom):
    seq = geom.seq
    t = jnp.arange(seq)
    row = (t // GRID_W).astype(F32)
    col = (t % GRID_W).astype(F32)
    freqs = 1.0 / (ROPE_THETA ** (jnp.arange(ROPE_PAIRS, dtype=F32) / ROPE_PAIRS))
    ang = jnp.concatenate([row[:, None] * freqs, col[:, None] * freqs], axis=-1)
    cos, sin = jnp.cos(ang), jnp.sin(ang)
    cos_l = jnp.tile(cos, (geom.bsz, 4))
    sin_l = jnp.tile(jnp.concatenate([-sin, -sin, sin, sin], axis=-1), (geom.bsz, 1))
    cos_t = jnp.concatenate([cos_l, jnp.ones((geom.n_ctx, LANES), F32)], axis=0)
    sin_t = jnp.concatenate([sin_l, jnp.zeros((geom.n_ctx, LANES), F32)], axis=0)
    return cos_t, sin_t


def _qk_col_perm():
    perm = []
    quarter = DA_HEAD_DIM // 2
    for h in range(DA_HEADS):
        for half in range(2):
            for m in range(2):
                base = h * 2 * DA_HEAD_DIM + m * DA_HEAD_DIM + half * quarter
                perm.extend(range(base, base + quarter))
    return jnp.asarray(perm, dtype=jnp.int32)


def kernel(x, c, ctx, c_ctx, w_ada, b_ada, norm1_w, norm2_w, ssd_w_in, ssd_conv_w, ssd_conv_b,
           ssd_a_log, ssd_dt_bias, ssd_d, ssd_norm_w, ssd_w_out, da_w_qkv, da_w_o, da_lq1, da_lk1,
           da_lq2, da_lk2, da_subln_w, ffn_w_gate, ffn_w_up, ffn_w_down, final_norm_w):
    bsz, seq, d = x.shape
    n_ctx_tok = ctx.shape[1]
    assert d == D_MODEL and bsz + 1 <= MOD_ROWS
    assert seq % CONV_ROWS == 0 and n_ctx_tok % CONV_ROWS == 0 and seq % GRID_W == 0
    geom = _Geom(bsz, seq, n_ctx_tok)

    x_all = jnp.concatenate([x.reshape(bsz * seq, d), ctx.reshape(bsz * n_ctx_tok, d)], axis=0)
    cvec = jnp.concatenate([c, c_ctx[None, :], jnp.zeros((MOD_ROWS - bsz - 1, d), F32)], axis=0)
    mod = _ada_call(cvec, w_ada.astype(BF16), b_ada)
    mod3 = mod.reshape(DEPTH * MOD_ROWS * 6, 1, D_MODEL)

    cos_t, sin_t = _rope_tables(geom)
    perm = _qk_col_perm()
    hq = DA_HEADS * 2 * DA_HEAD_DIM
    e_mat = (jnp.arange(D_INNER)[None, :] // SSD_HEAD_DIM == jnp.arange(SSD_HEADS)[:, None]).astype(BF16)

    out = None
    for i in range(DEPTH):
        j = i // 2
        last = i == DEPTH - 1
        n1 = norm1_w[i].reshape(1, d)
        if i % 2 == 0:
            w_in = ssd_w_in[j]
            w_zx = w_in[:, :D_ZX].astype(BF16)
            w_dt = w_in[:, D_ZX:].reshape(d, 2, SSD_HEADS).transpose(1, 0, 2)
            w_dt = jnp.pad(w_dt, ((0, 0), (0, 0), (0, LANES - SSD_HEADS))).astype(BF16)
            dt_b = jnp.pad(ssd_dt_bias[j], ((0, 0), (0, LANES - SSD_HEADS))).reshape(2, 1, LANES)
            zx, dt = _ssd_in_call(geom, i, x_all, mod3, n1, w_zx, w_dt, dt_b)
            xbc = _conv_call(geom, zx, ssd_conv_w[j], ssd_conv_b[j].reshape(1, D_CONV_CH))
            y = _ssd_scan_call(geom, xbc, dt, ssd_a_log[j].reshape(2, 1, SSD_HEADS), e_mat)
            d_skip_e = jnp.repeat(ssd_d[j], SSD_HEAD_DIM).reshape(1, D_INNER)
            x_all = _ssd_out_call(geom, i, y, xbc, zx, d_skip_e, ssd_norm_w[j].reshape(1, D_INNER),
                                  ssd_w_out[j].astype(BF16), x_all, mod3)
        else:
            lambda_init = 0.8 - 0.6 * math.exp(-0.3 * i)
            w_qkv = da_w_qkv[j]
            w_qkv = jnp.concatenate([w_qkv[:, :hq][:, perm], w_qkv[:, hq:2 * hq][:, perm],
                                     w_qkv[:, 2 * hq:]], axis=1).astype(BF16)
            q1, q2, k, v = _qkv_call(geom, i, x_all, mod3, n1, w_qkv, cos_t, sin_t)
            lam_vec = jnp.concatenate([da_lq1[j], da_lk1[j], da_lq2[j], da_lk2[j]]).reshape(1, 4 * DA_HEAD_DIM)
            subw = da_subln_w[j].reshape(1, LANES)
            o = jnp.zeros((geom.rows, D_MODEL), BF16)
            o = _flash_call(geom, q1, q2, k, v, lam_vec, subw, lambda_init, o, ctx_queries=False)
            if not last:
                o = _flash_call(geom, q1, q2, k, v, lam_vec, subw, lambda_init, o, ctx_queries=True)
            x_all = _attn_out_call(geom, i, o, da_w_o[j].astype(BF16), x_all, mod3)
        res = _ffn_call(geom, i, x_all, mod3, norm2_w[i].reshape(1, d), ffn_w_gate[i].astype(BF16),
                        ffn_w_up[i].astype(BF16), ffn_w_down[i].astype(BF16),
                        final_norm_w.reshape(1, d), final_norm=last)
        if last:
            out = res
        else:
            x_all = res
    return out.reshape(bsz, seq, d)
```

```python
import functools
import math

import jax
import jax.numpy as jnp
from jax import lax
from jax.experimental import pallas as pl
from jax.experimental.pallas import tpu as pltpu

F32 = jnp.float32
BF16 = jnp.bfloat16

D_MODEL = 1024
DEPTH = 4
GRID_W = 64
EPS = 1e-6

D_INNER = 2 * D_MODEL
SSD_HEAD_DIM = 64
SSD_HEADS = D_INNER // SSD_HEAD_DIM
SSD_GROUPS = 4
HEADS_PER_GROUP = SSD_HEADS // SSD_GROUPS
D_STATE = 128
CONV_W = 5
CHUNK = 128
D_BC = SSD_GROUPS * D_STATE
D_CONV_CH = D_INNER + 2 * D_BC
D_ZX = D_INNER + D_CONV_CH

DA_HEADS = D_MODEL // 128
DA_HEAD_DIM = 64
ROPE_THETA = 10000.0
ROPE_PAIRS = DA_HEAD_DIM // 4

D_FF = -(-8 * D_MODEL // (3 * 256)) * 256

LANES = 128
MOD_ROWS = 16
VMEM_LIMIT = 56 * 1024 * 1024


def _cparams(sem):
    return pltpu.CompilerParams(dimension_semantics=sem, vmem_limit_bytes=VMEM_LIMIT)


def _dot(a, b):
    return jnp.dot(a, b, preferred_element_type=F32)


def _dot_nt(a, b):
    return lax.dot_general(a, b, (((1,), (1,)), ((), ())), preferred_element_type=F32)


def _sigmoid(x):
    return 1.0 / (1.0 + jnp.exp(-x))


def _rms(xf, w):
    return xf * lax.rsqrt(jnp.mean(xf * xf, axis=-1, keepdims=True) + EPS) * w


def _resident(shape, index_map):
    return pl.BlockSpec(shape, index_map, pipeline_mode=pl.Buffered(1))


def _ada_kernel(c_ref, w_ref, b_ref, o_ref):
    cv = c_ref[...]
    s = (cv * _sigmoid(cv)).astype(BF16)
    o_ref[...] = _dot(s, w_ref[...]) + b_ref[...]


def _ada_call(cvec, w_ada, b_ada):
    depth, d, n = w_ada.shape
    tn = 2048
    return pl.pallas_call(
        _ada_kernel,
        grid=(depth, n // tn),
        in_specs=[
            pl.BlockSpec((MOD_ROWS, d), lambda l, j: (0, 0)),
            pl.BlockSpec((None, d, tn), lambda l, j: (l, 0, j)),
            pl.BlockSpec((None, 1, tn), lambda l, j: (l, 0, j)),
        ],
        out_specs=pl.BlockSpec((None, MOD_ROWS, tn), lambda l, j: (l, 0, j)),
        out_shape=jax.ShapeDtypeStruct((depth, MOD_ROWS, n), F32),
        compiler_params=_cparams(("arbitrary", "arbitrary")),
        name="adaln",
    )(cvec, w_ada, b_ada.reshape(depth, 1, n))


class _Geom:
    def __init__(self, bsz, seq, ctx):
        self.bsz, self.seq, self.ctx = bsz, seq, ctx
        self.n_lat = bsz * seq
        self.n_ctx = bsz * ctx
        self.rows = self.n_lat + self.n_ctx
        tm = 1024
        while seq % tm or self.n_ctx % tm:
            tm //= 2
        self.tm = tm

    def mod_spec(self, layer, k, tm):
        n_lat_tiles = self.n_lat // tm
        per_batch = self.seq // tm
        bsz = self.bsz

        def imap(i, *_):
            row = jnp.where(i < n_lat_tiles, i // per_batch, bsz)
            return ((layer * MOD_ROWS + row) * 6 + k, 0, 0)

        return pl.BlockSpec((None, 1, D_MODEL), imap)


def _softplus(x):
    return jnp.maximum(x, 0.0) + jnp.log1p(jnp.exp(-jnp.abs(x)))


def _ssd_in_kernel(x_ref, nw_ref, sh_ref, sc_ref, w_ref, wdt_ref, dtb_ref, zx_ref, dt_ref, a_scr):
    @pl.when(pl.program_id(1) == 0)
    def _():
        a = _rms(x_ref[...], nw_ref[...]) * (1.0 + sc_ref[...]) + sh_ref[...]
        ab = a.astype(BF16)
        a_scr[...] = ab
        for d in range(2):
            dt_ref[d] = _softplus(_dot(ab, wdt_ref[d]) + dtb_ref[d])

    zx_ref[...] = _dot(a_scr[...], w_ref[...]).astype(BF16)


def _ssd_in_call(geom, layer, x_all, mod3, norm_w, w_zx, w_dt, dt_bias):
    tm, tn = geom.tm, 1024
    rows = geom.rows
    return pl.pallas_call(
        _ssd_in_kernel,
        grid=(rows // tm, D_ZX // tn),
        in_specs=[
            pl.BlockSpec((tm, D_MODEL), lambda i, j: (i, 0)),
            pl.BlockSpec((1, D_MODEL), lambda i, j: (0, 0)),
            geom.mod_spec(layer, 0, tm),
            geom.mod_spec(layer, 1, tm),
            pl.BlockSpec((D_MODEL, tn), lambda i, j: (0, j)),
            pl.BlockSpec((2, D_MODEL, LANES), lambda i, j: (0, 0, 0)),
            pl.BlockSpec((2, 1, LANES), lambda i, j: (0, 0, 0)),
        ],
        out_specs=[
            pl.BlockSpec((tm, tn), lambda i, j: (i, j)),
            pl.BlockSpec((2, tm, LANES), lambda i, j: (0, i, 0)),
        ],
        out_shape=[
            jax.ShapeDtypeStruct((rows, D_ZX), BF16),
            jax.ShapeDtypeStruct((2, rows, LANES), F32),
        ],
        scratch_shapes=[pltpu.VMEM((tm, D_MODEL), BF16)],
        compiler_params=_cparams(("arbitrary", "arbitrary")),
        name="ssd_in_proj",
    )(x_all, norm_w, mod3, mod3, w_zx, w_dt, dt_bias)


CONV_ROWS = 256
HALO = 16


def _conv_kernel(cur_ref, prev_ref, next_ref, w_ref, b_ref, o_ref, *, tiles_lat, per_lat, per_ctx):
    i = pl.program_id(0)
    in_lat = i < tiles_lat
    pos = jnp.where(in_lat, i % per_lat, (i - tiles_lat) % per_ctx)
    last = jnp.where(in_lat, per_lat - 1, per_ctx - 1)
    keep_prev = (pos != 0).astype(F32)
    keep_next = (pos != last).astype(F32)
    cur = cur_ref[...].astype(F32)
    prev = prev_ref[...].astype(F32)[HALO - 8:, :] * keep_prev
    nxt = next_ref[...].astype(F32)[:8, :] * keep_next
    ext = jnp.concatenate([prev, cur, nxt], axis=0)
    tr = cur.shape[0]
    w = w_ref[...]
    acc = b_ref[...] + w[0:1, :] * ext[6:6 + tr, :]
    for k in range(1, CONV_W):
        acc = acc + w[k:k + 1, :] * ext[6 + k:6 + k + tr, :]
    o_ref[...] = (acc * _sigmoid(acc)).astype(BF16)


def _conv_call(geom, zx, conv_w, conv_b):
    tr = CONV_ROWS
    ct = 1024
    rows = geom.rows
    col0 = D_INNER // ct
    hb = tr // HALO
    n_halo_blocks = rows // HALO
    kern = functools.partial(_conv_kernel, tiles_lat=geom.n_lat // tr, per_lat=geom.seq // tr,
                             per_ctx=geom.ctx // tr)
    return pl.pallas_call(
        kern,
        grid=(rows // tr, D_CONV_CH // ct),
        in_specs=[
            pl.BlockSpec((tr, ct), lambda i, j: (i, col0 + j)),
            pl.BlockSpec((HALO, ct), lambda i, j: (jnp.maximum(i * hb - 1, 0), col0 + j)),
            pl.BlockSpec((HALO, ct), lambda i, j: (jnp.minimum((i + 1) * hb, n_halo_blocks - 1), col0 + j)),
            pl.BlockSpec((CONV_W, ct), lambda i, j: (0, j)),
            pl.BlockSpec((1, ct), lambda i, j: (0, j)),
        ],
        out_specs=pl.BlockSpec((tr, ct), lambda i, j: (i, j)),
        out_shape=jax.ShapeDtypeStruct((rows, D_CONV_CH), BF16),
        compiler_params=_cparams(("arbitrary", "arbitrary")),
        name="ssd_conv",
    )(zx, zx, zx, conv_w, conv_b)


def _split3(v):
    p1 = v.astype(BF16)
    r1 = v - p1.astype(F32)
    p2 = r1.astype(BF16)
    p3 = (r1 - p2.astype(F32)).astype(BF16)
    return p1, p2, p3


def _expand_heads(v, e_ref):
    p1, p2, p3 = _split3(v)
    e = e_ref[...]
    return _dot(p1, e) + _dot(p2, e) + _dot(p3, e)


def _ssd_scan_kernel(xs_ref, b_ref, c_ref, dt_ref, alog_ref, e_ref, y_ref, h_ref):
    d = pl.program_id(0)
    s = pl.program_id(2)

    @pl.when(s == 0)
    def _():
        h_ref[...] = jnp.zeros_like(h_ref)

    nh = SSD_HEADS
    dt = dt_ref[:, :nh]
    a = -jnp.exp(alog_ref[...])
    adt = dt * a
    li = lax.broadcasted_iota(jnp.int32, (CHUNK, CHUNK), 0)
    si = lax.broadcasted_iota(jnp.int32, (CHUNK, CHUNK), 1)
    sign = jnp.where(d == 0, 1, -1)
    mask = (si - li) * sign <= 0
    cum = jnp.dot(mask.astype(F32), adt, preferred_element_type=F32,
                  precision=lax.Precision.HIGHEST)
    tot = jnp.sum(adt, axis=0, keepdims=True)
    pad = jnp.zeros((CHUNK, CHUNK - 2 * nh), F32)
    tr = jnp.concatenate([cum, dt, pad], axis=1).T
    decay_out = jnp.exp(cum)
    w_state = dt * jnp.exp(tot - cum)
    cdec = jnp.broadcast_to(jnp.exp(tot), (8, nh))
    ex = _expand_heads(jnp.concatenate([w_state, decay_out, cdec], axis=0), e_ref)
    wst_e = ex[:CHUNK]
    dout_e = ex[CHUNK:2 * CHUNK]
    cdec_e = ex[2 * CHUNK:2 * CHUNK + 1]

    lane = lax.broadcasted_iota(jnp.int32, (CHUNK, 2 * SSD_HEAD_DIM), 1)
    gw = HEADS_PER_GROUP * SSD_HEAD_DIM
    for g in range(SSD_GROUPS):
        bg = b_ref[:, g * D_STATE:(g + 1) * D_STATE]
        cg = c_ref[:, g * D_STATE:(g + 1) * D_STATE]
        cb = _dot_nt(cg, bg)
        h_old = h_ref[:, g * gw:(g + 1) * gw]
        y_off = _dot(cg, h_old.astype(BF16)) * dout_e[:, g * gw:(g + 1) * gw]
        xg = xs_ref[:, g * gw:(g + 1) * gw]
        for q in range(HEADS_PER_GROUP // 2):
            ws = []
            for hh in range(2):
                h = g * HEADS_PER_GROUP + 2 * q + hh
                diff = cum[:, h:h + 1] - tr[h:h + 1, :]
                lmat = jnp.exp(jnp.where(mask, diff, -jnp.inf))
                ws.append((cb * lmat * tr[nh + h:nh + h + 1, :]).astype(BF16))
            xp = xg[:, q * 2 * SSD_HEAD_DIM:(q + 1) * 2 * SSD_HEAD_DIM]
            zero = jnp.zeros_like(xp)
            rhs = jnp.concatenate([jnp.where(lane < SSD_HEAD_DIM, xp, zero),
                                   jnp.where(lane >= SSD_HEAD_DIM, xp, zero)], axis=0)
            yd = _dot(jnp.concatenate(ws, axis=1), rhs)
            c0 = g * gw + q * 2 * SSD_HEAD_DIM
            y_ref[:, c0:c0 + 2 * SSD_HEAD_DIM] = yd + y_off[:, q * 2 * SSD_HEAD_DIM:(q + 1) * 2 * SSD_HEAD_DIM]
        xw = (xg.astype(F32) * wst_e[:, g * gw:(g + 1) * gw]).astype(BF16)
        bgt = bg.astype(F32).T.astype(BF16)
        h_ref[:, g * gw:(g + 1) * gw] = h_old * cdec_e[:, g * gw:(g + 1) * gw] + _dot(bgt, xw)


def _ssd_scan_call(geom, xbc, dt, a_log, e_mat):
    rows = geom.rows
    nc_lat = geom.seq // CHUNK
    nc_ctx = geom.ctx // CHUNK
    lat_blocks = geom.n_lat // CHUNK

    def chunk_block(d, b, s):
        in_ctx = s < nc_ctx
        c_ctx = jnp.where(d == 0, s, nc_ctx - 1 - s)
        sl = s - nc_ctx
        c_lat = jnp.where(d == 0, sl, nc_lat - 1 - sl)
        return jnp.where(in_ctx, lat_blocks + b * nc_ctx + c_ctx, b * nc_lat + c_lat)

    xcols = D_INNER // D_BC
    return pl.pallas_call(
        _ssd_scan_kernel,
        grid=(2, geom.bsz, nc_ctx + nc_lat),
        in_specs=[
            pl.BlockSpec((CHUNK, D_INNER), lambda d, b, s: (chunk_block(d, b, s), 0)),
            pl.BlockSpec((CHUNK, D_BC), lambda d, b, s: (chunk_block(d, b, s), xcols)),
            pl.BlockSpec((CHUNK, D_BC), lambda d, b, s: (chunk_block(d, b, s), xcols + 1)),
            pl.BlockSpec((None, CHUNK, LANES), lambda d, b, s: (d, chunk_block(d, b, s), 0)),
            pl.BlockSpec((None, 1, SSD_HEADS), lambda d, b, s: (d, 0, 0)),
            pl.BlockSpec((SSD_HEADS, D_INNER), lambda d, b, s: (0, 0)),
        ],
        out_specs=pl.BlockSpec((None, CHUNK, D_INNER), lambda d, b, s: (d, chunk_block(d, b, s), 0)),
        out_shape=jax.ShapeDtypeStruct((2, rows, D_INNER), F32),
        scratch_shapes=[pltpu.VMEM((D_STATE, D_INNER), F32)],
        compiler_params=_cparams(("arbitrary", "arbitrary", "arbitrary")),
        name="ssd_scan",
    )(xbc, xbc, xbc, dt, a_log, e_mat)


def _ssd_out_kernel(y_ref, xs_ref, z_ref, dsk_ref, gw_ref, w_ref, x_ref, gate_ref, o_ref):
    y = y_ref[0] + y_ref[1] + xs_ref[...].astype(F32) * dsk_ref[...]
    z = z_ref[...].astype(F32)
    yn = _rms(y * (z * _sigmoid(z)), gw_ref[...]).astype(BF16)
    o_ref[...] = x_ref[...] + gate_ref[...] * _dot(yn, w_ref[...])


def _ssd_out_call(geom, layer, y, xbc, zx, d_skip_e, gnorm_w, w_out, x_all, mod3):
    tm = min(geom.tm, 512)
    rows = geom.rows
    return pl.pallas_call(
        _ssd_out_kernel,
        grid=(rows // tm,),
        in_specs=[
            pl.BlockSpec((2, tm, D_INNER), lambda i: (0, i, 0)),
            pl.BlockSpec((tm, D_INNER), lambda i: (i, 0)),
            pl.BlockSpec((tm, D_INNER), lambda i: (i, 0)),
            pl.BlockSpec((1, D_INNER), lambda i: (0, 0)),
            pl.BlockSpec((1, D_INNER), lambda i: (0, 0)),
            _resident((D_INNER, D_MODEL), lambda i: (0, 0)),
            pl.BlockSpec((tm, D_MODEL), lambda i: (i, 0)),
            geom.mod_spec(layer, 2, tm),
        ],
        out_specs=pl.BlockSpec((tm, D_MODEL), lambda i: (i, 0)),
        out_shape=jax.ShapeDtypeStruct((rows, D_MODEL), F32),
        compiler_params=_cparams(("arbitrary",)),
        name="ssd_out_proj",
    )(y, xbc, zx, d_skip_e, gnorm_w, w_out, x_all, mod3)


def _rope_blocks(acc, cos, sin):
    outs = []
    for c in range(acc.shape[1] // LANES):
        xb = acc[:, c * LANES:(c + 1) * LANES]
        outs.append(xb * cos + pltpu.roll(xb, LANES // 2, 1) * sin)
    return outs


def _qkv_kernel(x_ref, nw_ref, sh_ref, sc_ref, w_ref, cos_ref, sin_ref,
                q1_ref, q2_ref, k_ref, v_ref, a_scr):
    j = pl.program_id(1)

    @pl.when(j == 0)
    def _():
        a = _rms(x_ref[...], nw_ref[...]) * (1.0 + sc_ref[...]) + sh_ref[...]
        a_scr[...] = a.astype(BF16)

    acc = _dot(a_scr[...], w_ref[...])

    @pl.when(j == 0)
    def _():
        scale = DA_HEAD_DIM ** -0.5 * math.log2(math.e)
        lane = lax.broadcasted_iota(jnp.int32, (acc.shape[0], LANES), 1)
        is_map1 = (lane % (LANES // 2)) < (LANES // 4)
        for c, blk in enumerate(_rope_blocks(acc, cos_ref[...], sin_ref[...])):
            blk = blk * scale
            q1_ref[:, c * LANES:(c + 1) * LANES] = jnp.where(is_map1, blk, 0.0).astype(BF16)
            q2_ref[:, c * LANES:(c + 1) * LANES] = jnp.where(is_map1, 0.0, blk).astype(BF16)

    @pl.when(j == 1)
    def _():
        for c, blk in enumerate(_rope_blocks(acc, cos_ref[...], sin_ref[...])):
            k_ref[:, c * LANES:(c + 1) * LANES] = blk.astype(BF16)

    @pl.when(j == 2)
    def _():
        v_ref[...] = acc.astype(BF16)


def _qkv_call(geom, layer, x_all, mod3, norm_w, w_qkv, cos_t, sin_t):
    tm = geom.tm
    rows = geom.rows
    row_out = pl.BlockSpec((tm, D_MODEL), lambda i, j: (i, 0))
    return pl.pallas_call(
        _qkv_kernel,
        grid=(rows // tm, 3),
        in_specs=[
            pl.BlockSpec((tm, D_MODEL), lambda i, j: (i, 0)),
            pl.BlockSpec((1, D_MODEL), lambda i, j: (0, 0)),
            geom.mod_spec(layer, 0, tm),
            geom.mod_spec(layer, 1, tm),
            pl.BlockSpec((D_MODEL, D_MODEL), lambda i, j: (0, j)),
            pl.BlockSpec((tm, LANES), lambda i, j: (i, 0)),
            pl.BlockSpec((tm, LANES), lambda i, j: (i, 0)),
        ],
        out_specs=[row_out, row_out, row_out, row_out],
        out_shape=[jax.ShapeDtypeStruct((rows, D_MODEL), BF16)] * 4,
        scratch_shapes=[pltpu.VMEM((tm, D_MODEL), BF16)],
        compiler_params=_cparams(("arbitrary", "arbitrary")),
        name="attn_qkv_rope",
    )(x_all, norm_w, mod3, mod3, w_qkv, cos_t, sin_t)


FLASH_ROWS = 64


def _flash_kernel(lam_ref, subw_ref, q1_ref, q2_ref, *refs, seg_lens, tk, lambda_init):
    n_seg = len(seg_lens)
    kv_refs = refs[:2 * n_seg]
    o_ref, s_scr, p_scr, acc_scr, m_scr, al_scr = refs[2 * n_seg + 1:]
    tq = q1_ref.shape[0]
    qs = (q1_ref[...], q2_ref[...])
    acc_scr[...] = jnp.zeros_like(acc_scr)
    m_scr[...] = jnp.full_like(m_scr, -jnp.inf)

    def scores(slot, k_blk):
        w = k_blk.shape[0]
        for t in range(2):
            s_scr[slot, t, :, :w] = _dot_nt(qs[t], k_blk)

    def absorb(slot, v_blk):
        w = v_blk.shape[0]
        vaug = jnp.concatenate([v_blk, jnp.ones_like(v_blk)], axis=1)
        for t in range(2):
            for r in range(0, tq, FLASH_ROWS):
                rows = pl.ds(r, FLASH_ROWS)
                tiles = [s_scr[slot, t, rows, c:c + LANES] for c in range(0, w, LANES)]
                mx = tiles[0]
                for tl in tiles[1:]:
                    mx = jnp.maximum(mx, tl)
                m_old = m_scr[t, rows, :]
                m_new = jnp.maximum(m_old, jnp.max(mx, axis=1, keepdims=True))
                for ci, tl in enumerate(tiles):
                    p_scr[t, rows, ci * LANES:(ci + 1) * LANES] = jnp.exp2(tl - m_new).astype(BF16)
                m_scr[t, rows, :] = m_new
                al_scr[t, rows, :] = jnp.exp2(m_old - m_new)
            alpha = al_scr[t]
            acc_scr[t] = (jnp.concatenate([alpha, alpha], axis=1) * acc_scr[t]
                          + _dot(p_scr[t, :, :w], vaug))

    if n_seg == 1:
        scores(0, kv_refs[0][...])
        absorb(0, kv_refs[1][...])
    else:
        kl_ref, vl_ref, kc_ref, vc_ref = kv_refs
        n_lat = seg_lens[0] // tk

        def k_block(i):
            if isinstance(i, int):
                return kc_ref[...] if i == n_lat else kl_ref[i * tk:(i + 1) * tk, :]
            return kl_ref[pl.ds(pl.multiple_of(i * tk, tk), tk), :]

        def v_block(i):
            if isinstance(i, int):
                return vc_ref[...] if i == n_lat else vl_ref[i * tk:(i + 1) * tk, :]
            return vl_ref[pl.ds(pl.multiple_of(i * tk, tk), tk), :]

        def step(i, slot):
            if not isinstance(i, int) or i + 1 <= n_lat:
                scores(1 - slot, k_block(i + 1))
            absorb(slot, v_block(i))

        scores(0, k_block(0))
        pairs = (n_lat - 1) // 2

        def body(jj, carry):
            step(2 * jj, 0)
            step(2 * jj + 1, 1)
            return carry

        lax.fori_loop(0, pairs, body, 0)
        for i in range(2 * pairs, n_lat + 1):
            step(i, i % 2)

    lam_v = lam_ref[...]
    hd = DA_HEAD_DIM
    lam = (jnp.exp(jnp.sum(lam_v[:, 0:hd] * lam_v[:, hd:2 * hd], axis=1, keepdims=True))
           - jnp.exp(jnp.sum(lam_v[:, 2 * hd:3 * hd] * lam_v[:, 3 * hd:4 * hd], axis=1, keepdims=True))
           + lambda_init)
    a1 = acc_scr[0]
    a2 = acc_scr[1]
    o = a1[:, :LANES] / a1[:, LANES:] - lam * (a2[:, :LANES] / a2[:, LANES:])
    o = _rms(o, subw_ref[...]) * (1.0 - lambda_init)
    o_ref[...] = o.astype(BF16)


def _flash_call(geom, q1, q2, k, v, lam_vec, subln_w, lambda_init, o_prev, ctx_queries):
    bsz, seq, ctx = geom.bsz, geom.seq, geom.ctx
    tk = 512
    if ctx_queries:
        tq = min(256, ctx)
        nq = ctx // tq
        q_base = geom.n_lat // tq
        seg_lens = (ctx,)
    else:
        tq = min(256, seq)
        nq = seq // tq
        q_base = 0
        seg_lens = (seq, ctx)
    ctx_base = geom.n_lat // ctx

    def q_map(b, h, i):
        return (q_base + b * nq + i, h)

    kv_lat = pl.BlockSpec((seq, LANES), lambda b, h, i: (b, h))
    kv_ctx = pl.BlockSpec((ctx, LANES), lambda b, h, i: (ctx_base + b, h))
    kv_specs = [kv_ctx, kv_ctx] if ctx_queries else [kv_lat, kv_lat, kv_ctx, kv_ctx]
    kv_args = [k, v] if ctx_queries else [k, v, k, v]
    kern = functools.partial(_flash_kernel, seg_lens=seg_lens, tk=tk, lambda_init=lambda_init)
    n_in = 4 + len(kv_args)
    return pl.pallas_call(
        kern,
        grid=(bsz, DA_HEADS, nq),
        in_specs=[
            pl.BlockSpec((1, 4 * DA_HEAD_DIM), lambda b, h, i: (0, 0)),
            pl.BlockSpec((1, LANES), lambda b, h, i: (0, 0)),
            pl.BlockSpec((tq, LANES), q_map),
            pl.BlockSpec((tq, LANES), q_map),
            *kv_specs,
            pl.BlockSpec(memory_space=pl.ANY),
        ],
        out_specs=pl.BlockSpec((tq, LANES), q_map),
        out_shape=jax.ShapeDtypeStruct((geom.rows, D_MODEL), BF16),
        input_output_aliases={n_in: 0},
        scratch_shapes=[pltpu.VMEM((2, 2, tq, tk), F32),
                        pltpu.VMEM((2, tq, tk), BF16),
                        pltpu.VMEM((2, tq, 2 * LANES), F32),
                        pltpu.VMEM((2, tq, LANES), F32),
                        pltpu.VMEM((2, tq, LANES), F32)],
        compiler_params=_cparams(("arbitrary", "arbitrary", "arbitrary")),
        name="flash_ctx" if ctx_queries else "flash_lat",
    )(lam_vec, subln_w, q1, q2, *kv_args, o_prev)


def _attn_out_kernel(o_ref, w_ref, x_ref, gate_ref, out_ref):
    out_ref[...] = x_ref[...] + gate_ref[...] * _dot(o_ref[...], w_ref[...])


def _attn_out_call(geom, layer, o, w_o, x_all, mod3):
    tm = geom.tm
    rows = geom.rows
    return pl.pallas_call(
        _attn_out_kernel,
        grid=(rows // tm,),
        in_specs=[
            pl.BlockSpec((tm, D_MODEL), lambda i: (i, 0)),
            _resident((D_MODEL, D_MODEL), lambda i: (0, 0)),
            pl.BlockSpec((tm, D_MODEL), lambda i: (i, 0)),
            geom.mod_spec(layer, 2, tm),
        ],
        out_specs=pl.BlockSpec((tm, D_MODEL), lambda i: (i, 0)),
        out_shape=jax.ShapeDtypeStruct((rows, D_MODEL), F32),
        compiler_params=_cparams(("arbitrary",)),
        name="attn_out_proj",
    )(o, w_o, x_all, mod3)


FFN_TILE = 256


def _ffn_kernel(x_ref, nw_ref, sh_ref, sc_ref, gate_ref, wg_ref, wu_ref, wd_ref, fw_ref,
                o_ref, h_scr, *, final_norm):
    x = x_ref[...]
    a = (_rms(x, nw_ref[...]) * (1.0 + sc_ref[...]) + sh_ref[...]).astype(BF16)
    for f in range(0, D_FF, FFN_TILE):
        g = _dot(a, wg_ref[:, f:f + FFN_TILE])
        u = _dot(a, wu_ref[:, f:f + FFN_TILE])
        h_scr[:, f:f + FFN_TILE] = (g * _sigmoid(g) * u).astype(BF16)
    out = x + gate_ref[...] * _dot(h_scr[...], wd_ref[...])
    if final_norm:
        out = _rms(out, fw_ref[...])
    o_ref[...] = out


def _ffn_call(geom, layer, x_all, mod3, norm_w, w_gate, w_up, w_down, final_w, final_norm):
    tm = min(geom.tm, 512)
    rows = geom.n_lat if final_norm else geom.rows
    kern = functools.partial(_ffn_kernel, final_norm=final_norm)
    return pl.pallas_call(
        kern,
        grid=(rows // tm,),
        in_specs=[
            pl.BlockSpec((tm, D_MODEL), lambda i: (i, 0)),
            pl.BlockSpec((1, D_MODEL), lambda i: (0, 0)),
            geom.mod_spec(layer, 3, tm),
            geom.mod_spec(layer, 4, tm),
            geom.mod_spec(layer, 5, tm),
            _resident((D_MODEL, D_FF), lambda i: (0, 0)),
            _resident((D_MODEL, D_FF), lambda i: (0, 0)),
            _resident((D_FF, D_MODEL), lambda i: (0, 0)),
            pl.BlockSpec((1, D_MODEL), lambda i: (0, 0)),
        ],
        out_specs=pl.BlockSpec((tm, D_MODEL), lambda i: (i, 0)),
        out_shape=jax.ShapeDtypeStruct((rows, D_MODEL), F32),
        scratch_shapes=[pltpu.VMEM((tm, D_FF), BF16)],
        compiler_params=_cparams(("arbitrary",)),
        name="ffn_swiglu",
    )(x_all, norm_w, mod3, mod3, mod3, w_gate, w_up, w_down, final_w)


def _rope_tables(geom):
    seq = geom.seq
    t = jnp.arange(seq)
    row = (t // GRID_W).astype(F32)
    col = (t % GRID_W).astype(F32)
    freqs = 1.0 / (ROPE_THETA ** (jnp.arange(ROPE_PAIRS, dtype=F32) / ROPE_PAIRS))
    ang = jnp.concatenate([row[:, None] * freqs, col[:, None] * freqs], axis=-1)
    cos, sin = jnp.cos(ang), jnp.sin(ang)
    cos_l = jnp.tile(cos, (geom.bsz, 4))
    sin_l = jnp.tile(jnp.concatenate([-sin, -sin, sin, sin], axis=-1), (geom.bsz, 1))
    cos_t = jnp.concatenate([cos_l, jnp.ones((geom.n_ctx, LANES), F32)], axis=0)
    sin_t = jnp.concatenate([sin_l, jnp.zeros((geom.n_ctx, LANES), F32)], axis=0)
    return cos_t, sin_t


def _qk_col_perm():
    perm = []
    quarter = DA_HEAD_DIM // 2
    for h in range(DA_HEADS):
        for half in range(2):
            for m in range(2):
                base = h * 2 * DA_HEAD_DIM + m * DA_HEAD_DIM + half * quarter
                perm.extend(range(base, base + quarter))
    return jnp.asarray(perm, dtype=jnp.int32)


def kernel(x, c, ctx, c_ctx, w_ada, b_ada, norm1_w, norm2_w, ssd_w_in, ssd_conv_w, ssd_conv_b,
           ssd_a_log, ssd_dt_bias, ssd_d, ssd_norm_w, ssd_w_out, da_w_qkv, da_w_o, da_lq1, da_lk1,
           da_lq2, da_lk2, da_subln_w, ffn_w_gate, ffn_w_up, ffn_w_down, final_norm_w):
    bsz, seq, d = x.shape
    n_ctx_tok = ctx.shape[1]
    assert d == D_MODEL and bsz + 1 <= MOD_ROWS
    assert seq % CONV_ROWS == 0 and n_ctx_tok % CONV_ROWS == 0 and seq % GRID_W == 0
    geom = _Geom(bsz, seq, n_ctx_tok)

    x_all = jnp.concatenate([x.reshape(bsz * seq, d), ctx.reshape(bsz * n_ctx_tok, d)], axis=0)
    cvec = jnp.concatenate([c, c_ctx[None, :], jnp.zeros((MOD_ROWS - bsz - 1, d), F32)], axis=0)
    mod = _ada_call(cvec, w_ada.astype(BF16), b_ada)
    mod3 = mod.reshape(DEPTH * MOD_ROWS * 6, 1, D_MODEL)

    cos_t, sin_t = _rope_tables(geom)
    perm = _qk_col_perm()
    hq = DA_HEADS * 2 * DA_HEAD_DIM
    e_mat = (jnp.arange(D_INNER)[None, :] // SSD_HEAD_DIM == jnp.arange(SSD_HEADS)[:, None]).astype(BF16)

    out = None
    for i in range(DEPTH):
        j = i // 2
        last = i == DEPTH - 1
        n1 = norm1_w[i].reshape(1, d)
        if i % 2 == 0:
            w_in = ssd_w_in[j]
            w_zx = w_in[:, :D_ZX].astype(BF16)
            w_dt = w_in[:, D_ZX:].reshape(d, 2, SSD_HEADS).transpose(1, 0, 2)
            w_dt = jnp.pad(w_dt, ((0, 0), (0, 0), (0, LANES - SSD_HEADS))).astype(BF16)
            dt_b = jnp.pad(ssd_dt_bias[j], ((0, 0), (0, LANES - SSD_HEADS))).reshape(2, 1, LANES)
            zx, dt = _ssd_in_call(geom, i, x_all, mod3, n1, w_zx, w_dt, dt_b)
            xbc = _conv_call(geom, zx, ssd_conv_w[j], ssd_conv_b[j].reshape(1, D_CONV_CH))
            y = _ssd_scan_call(geom, xbc, dt, ssd_a_log[j].reshape(2, 1, SSD_HEADS), e_mat)
            d_skip_e = jnp.repeat(ssd_d[j], SSD_HEAD_DIM).reshape(1, D_INNER)
            x_all = _ssd_out_call(geom, i, y, xbc, zx, d_skip_e, ssd_norm_w[j].reshape(1, D_INNER),
                                  ssd_w_out[j].astype(BF16), x_all, mod3)
        else:
            lambda_init = 0.8 - 0.6 * math.exp(-0.3 * i)
            w_qkv = da_w_qkv[j]
            w_qkv = jnp.concatenate([w_qkv[:, :hq][:, perm], w_qkv[:, hq:2 * hq][:, perm],
                                     w_qkv[:, 2 * hq:]], axis=1).astype(BF16)
            q1, q2, k, v = _qkv_call(geom, i, x_all, mod3, n1, w_qkv, cos_t, sin_t)
            lam_vec = jnp.concatenate([da_lq1[j], da_lk1[j], da_lq2[j], da_lk2[j]]).reshape(1, 4 * DA_HEAD_DIM)
            subw = da_subln_w[j].reshape(1, LANES)
            o = jnp.zeros((geom.rows, D_MODEL), BF16)
            o = _flash_call(geom, q1, q2, k, v, lam_vec, subw, lambda_init, o, ctx_queries=False)
            if not last:
                o = _flash_call(geom, q1, q2, k, v, lam_vec, subw, lambda_init, o, ctx_queries=True)
            x_all = _attn_out_call(geom, i, o, da_w_o[j].astype(BF16), x_all, mod3)
        res = _ffn_call(geom, i, x_all, mod3, norm2_w[i].reshape(1, d), ffn_w_gate[i].astype(BF16),
                        ffn_w_up[i].astype(BF16), ffn_w_down[i].astype(BF16),
                        final_norm_w.reshape(1, d), final_norm=last)
        if last:
            out = res
        else:
            x_all = res
    return out.reshape(bsz, seq, d)
```

```python
import functools
import math

import jax
import jax.numpy as jnp
from jax import lax
from jax.experimental import pallas as pl
from jax.experimental.pallas import tpu as pltpu

F32 = jnp.float32
BF16 = jnp.bfloat16

D_MODEL = 1024
DEPTH = 4
GRID_W = 64
EPS = 1e-6

D_INNER = 2 * D_MODEL
SSD_HEAD_DIM = 64
SSD_HEADS = D_INNER // SSD_HEAD_DIM
SSD_GROUPS = 4
HEADS_PER_GROUP = SSD_HEADS // SSD_GROUPS
D_STATE = 128
CONV_W = 5
CHUNK = 128
D_BC = SSD_GROUPS * D_STATE
D_CONV_CH = D_INNER + 2 * D_BC
D_ZX = D_INNER + D_CONV_CH

DA_HEADS = D_MODEL // 128
DA_HEAD_DIM = 64
ROPE_THETA = 10000.0
ROPE_PAIRS = DA_HEAD_DIM // 4

D_FF = -(-8 * D_MODEL // (3 * 256)) * 256

LANES = 128
LOG2E = math.log2(math.e)
MOD_ROWS = 16
VMEM_LIMIT = 56 * 1024 * 1024


def _cparams(sem):
    return pltpu.CompilerParams(dimension_semantics=sem, vmem_limit_bytes=VMEM_LIMIT)


def _dot(a, b):
    return jnp.dot(a, b, preferred_element_type=F32)


def _dot_nt(a, b):
    return lax.dot_general(a, b, (((1,), (1,)), ((), ())), preferred_element_type=F32)


def _sigmoid(x):
    return 1.0 / (1.0 + jnp.exp(-x))


def _rms(xf, w):
    return xf * lax.rsqrt(jnp.mean(xf * xf, axis=-1, keepdims=True) + EPS) * w


def _resident(shape, index_map):
    return pl.BlockSpec(shape, index_map, pipeline_mode=pl.Buffered(1))


def _ada_kernel(c_ref, w_ref, b_ref, o_ref):
    cv = c_ref[...]
    s = (cv * _sigmoid(cv)).astype(BF16)
    o_ref[...] = _dot(s, w_ref[...]) + b_ref[...]


def _ada_call(cvec, w_ada, b_ada):
    depth, d, n = w_ada.shape
    tn = 2048
    return pl.pallas_call(
        _ada_kernel,
        grid=(depth, n // tn),
        in_specs=[
            pl.BlockSpec((MOD_ROWS, d), lambda l, j: (0, 0)),
            pl.BlockSpec((None, d, tn), lambda l, j: (l, 0, j)),
            pl.BlockSpec((None, 1, tn), lambda l, j: (l, 0, j)),
        ],
        out_specs=pl.BlockSpec((None, MOD_ROWS, tn), lambda l, j: (l, 0, j)),
        out_shape=jax.ShapeDtypeStruct((depth, MOD_ROWS, n), F32),
        compiler_params=_cparams(("arbitrary", "arbitrary")),
        name="adaln",
    )(cvec, w_ada, b_ada.reshape(depth, 1, n))


class _Geom:
    def __init__(self, bsz, seq, ctx):
        self.bsz, self.seq, self.ctx = bsz, seq, ctx
        self.n_lat = bsz * seq
        self.n_ctx = bsz * ctx
        self.rows = self.n_lat + self.n_ctx
        tm = 1024
        while seq % tm or self.n_ctx % tm:
            tm //= 2
        self.tm = tm

    def mod_spec(self, layer, k, tm):
        n_lat_tiles = self.n_lat // tm
        per_batch = self.seq // tm
        bsz = self.bsz

        def imap(i, *_):
            row = jnp.where(i < n_lat_tiles, i // per_batch, bsz)
            return ((layer * MOD_ROWS + row) * 6 + k, 0, 0)

        return pl.BlockSpec((None, 1, D_MODEL), imap)


def _softplus(x):
    return jnp.maximum(x, 0.0) + jnp.log1p(jnp.exp(-jnp.abs(x)))


def _ssd_in_kernel(x_ref, nw_ref, sh_ref, sc_ref, w_ref, wdt_ref, dtb_ref, zx_ref, dt_ref, a_scr):
    @pl.when(pl.program_id(1) == 0)
    def _():
        a = _rms(x_ref[...], nw_ref[...]) * (1.0 + sc_ref[...]) + sh_ref[...]
        ab = a.astype(BF16)
        a_scr[...] = ab
        for d in range(2):
            dt_ref[d] = _softplus(_dot(ab, wdt_ref[d]) + dtb_ref[d])

    zx_ref[...] = _dot(a_scr[...], w_ref[...]).astype(BF16)


def _ssd_in_call(geom, layer, x_all, mod3, norm_w, w_zx, w_dt, dt_bias):
    tm, tn = geom.tm, 1024
    rows = geom.rows
    return pl.pallas_call(
        _ssd_in_kernel,
        grid=(rows // tm, D_ZX // tn),
        in_specs=[
            pl.BlockSpec((tm, D_MODEL), lambda i, j: (i, 0)),
            pl.BlockSpec((1, D_MODEL), lambda i, j: (0, 0)),
            geom.mod_spec(layer, 0, tm),
            geom.mod_spec(layer, 1, tm),
            pl.BlockSpec((D_MODEL, tn), lambda i, j: (0, j)),
            pl.BlockSpec((2, D_MODEL, LANES), lambda i, j: (0, 0, 0)),
            pl.BlockSpec((2, 1, LANES), lambda i, j: (0, 0, 0)),
        ],
        out_specs=[
            pl.BlockSpec((tm, tn), lambda i, j: (i, j)),
            pl.BlockSpec((2, tm, LANES), lambda i, j: (0, i, 0)),
        ],
        out_shape=[
            jax.ShapeDtypeStruct((rows, D_ZX), BF16),
            jax.ShapeDtypeStruct((2, rows, LANES), F32),
        ],
        scratch_shapes=[pltpu.VMEM((tm, D_MODEL), BF16)],
        compiler_params=_cparams(("arbitrary", "arbitrary")),
        name="ssd_in_proj",
    )(x_all, norm_w, mod3, mod3, w_zx, w_dt, dt_bias)


CONV_ROWS = 256
HALO = 16


def _conv_kernel(cur_ref, prev_ref, next_ref, w_ref, b_ref, o_ref, *, tiles_lat, per_lat, per_ctx):
    i = pl.program_id(0)
    in_lat = i < tiles_lat
    pos = jnp.where(in_lat, i % per_lat, (i - tiles_lat) % per_ctx)
    last = jnp.where(in_lat, per_lat - 1, per_ctx - 1)
    keep_prev = (pos != 0).astype(F32)
    keep_next = (pos != last).astype(F32)
    cur = cur_ref[...].astype(F32)
    prev = prev_ref[...].astype(F32)[HALO - 8:, :] * keep_prev
    nxt = next_ref[...].astype(F32)[:8, :] * keep_next
    ext = jnp.concatenate([prev, cur, nxt], axis=0)
    tr = cur.shape[0]
    w = w_ref[...]
    acc = b_ref[...] + w[0:1, :] * ext[6:6 + tr, :]
    for k in range(1, CONV_W):
        acc = acc + w[k:k + 1, :] * ext[6 + k:6 + k + tr, :]
    o_ref[...] = (acc * _sigmoid(acc)).astype(BF16)


def _conv_call(geom, zx, conv_w, conv_b):
    tr = CONV_ROWS
    ct = 1024
    rows = geom.rows
    col0 = D_INNER // ct
    hb = tr // HALO
    n_halo_blocks = rows // HALO
    kern = functools.partial(_conv_kernel, tiles_lat=geom.n_lat // tr, per_lat=geom.seq // tr,
                             per_ctx=geom.ctx // tr)
    return pl.pallas_call(
        kern,
        grid=(rows // tr, D_CONV_CH // ct),
        in_specs=[
            pl.BlockSpec((tr, ct), lambda i, j: (i, col0 + j)),
            pl.BlockSpec((HALO, ct), lambda i, j: (jnp.maximum(i * hb - 1, 0), col0 + j)),
            pl.BlockSpec((HALO, ct), lambda i, j: (jnp.minimum((i + 1) * hb, n_halo_blocks - 1), col0 + j)),
            pl.BlockSpec((CONV_W, ct), lambda i, j: (0, j)),
            pl.BlockSpec((1, ct), lambda i, j: (0, j)),
        ],
        out_specs=pl.BlockSpec((tr, ct), lambda i, j: (i, j)),
        out_shape=jax.ShapeDtypeStruct((rows, D_CONV_CH), BF16),
        compiler_params=_cparams(("arbitrary", "arbitrary")),
        name="ssd_conv",
    )(zx, zx, zx, conv_w, conv_b)


def _split3(v):
    p1 = v.astype(BF16)
    r1 = v - p1.astype(F32)
    p2 = r1.astype(BF16)
    p3 = (r1 - p2.astype(F32)).astype(BF16)
    return p1, p2, p3


def _expand_heads(v, e2_ref):
    hi = v.astype(BF16)
    lo = (v - hi.astype(F32)).astype(BF16)
    return _dot(jnp.concatenate([hi, lo], axis=1), e2_ref[...])


def _scan_chunk(backward, xs_ref, b_ref, c_ref, dt_ref, alog, e_ref, y_ref, h_ref):
    nh = SSD_HEADS
    dt = dt_ref[:, :nh]
    a = -jnp.exp(alog)
    adt = dt * (a * LOG2E)
    li = lax.broadcasted_iota(jnp.int32, (CHUNK, CHUNK), 0)
    si = lax.broadcasted_iota(jnp.int32, (CHUNK, CHUNK), 1)
    mask = si >= li if backward else si <= li
    cum3 = _dot(mask.astype(F32).astype(BF16), jnp.concatenate(_split3(adt), axis=1))
    cum = cum3[:, :nh] + cum3[:, nh:2 * nh] + cum3[:, 2 * nh:3 * nh]
    tot = jnp.sum(adt, axis=0, keepdims=True)
    pad = jnp.zeros((CHUNK, CHUNK - nh), F32)
    tr = jnp.concatenate([cum - jnp.log2(dt), pad], axis=1).T
    decay_out = jnp.exp2(cum)
    w_state = dt * jnp.exp2(tot - cum)
    cdec = jnp.broadcast_to(jnp.exp2(tot), (8, nh))
    ex = _expand_heads(jnp.concatenate([w_state, decay_out, cdec], axis=0), e_ref)
    wst_e = ex[:CHUNK]
    dout_e = ex[CHUNK:2 * CHUNK]
    cdec_e = ex[2 * CHUNK:2 * CHUNK + 1]

    lane = lax.broadcasted_iota(jnp.int32, (CHUNK, 2 * SSD_HEAD_DIM), 1)
    gw = HEADS_PER_GROUP * SSD_HEAD_DIM
    for g in range(SSD_GROUPS):
        bg = b_ref[:, g * D_STATE:(g + 1) * D_STATE]
        cg = c_ref[:, g * D_STATE:(g + 1) * D_STATE]
        cb = _dot_nt(cg, bg)
        h_old = h_ref[:, g * gw:(g + 1) * gw]
        y_off = _dot(cg, h_old.astype(BF16)) * dout_e[:, g * gw:(g + 1) * gw]
        xg = xs_ref[:, g * gw:(g + 1) * gw]
        for q in range(HEADS_PER_GROUP // 2):
            ws = []
            for hh in range(2):
                h = g * HEADS_PER_GROUP + 2 * q + hh
                diff = cum[:, h:h + 1] - tr[h:h + 1, :]
                lmat = jnp.exp2(jnp.where(mask, diff, -jnp.inf))
                ws.append((cb * lmat).astype(BF16))
            xp = xg[:, q * 2 * SSD_HEAD_DIM:(q + 1) * 2 * SSD_HEAD_DIM]
            zero = jnp.zeros_like(xp)
            rhs = jnp.concatenate([jnp.where(lane < SSD_HEAD_DIM, xp, zero),
                                   jnp.where(lane >= SSD_HEAD_DIM, xp, zero)], axis=0)
            yd = _dot(jnp.concatenate(ws, axis=1), rhs)
            c0 = g * gw + q * 2 * SSD_HEAD_DIM
            y_ref[:, c0:c0 + 2 * SSD_HEAD_DIM] = (
                yd + y_off[:, q * 2 * SSD_HEAD_DIM:(q + 1) * 2 * SSD_HEAD_DIM]).astype(y_ref.dtype)
        xw = (xg.astype(F32) * wst_e[:, g * gw:(g + 1) * gw]).astype(BF16)
        bgt = bg.astype(F32).T.astype(BF16)
        h_ref[:, g * gw:(g + 1) * gw] = h_old * cdec_e[:, g * gw:(g + 1) * gw] + _dot(bgt, xw)


def _ssd_scan_kernel(xf_ref, bf_ref, cf_ref, dtf_ref, xb_ref, bb_ref, cb_ref, dtb_ref, alog_ref, e_ref,
                     yf_ref, yb_ref, h_ref):
    @pl.when(pl.program_id(1) == 0)
    def _():
        h_ref[...] = jnp.zeros_like(h_ref)

    _scan_chunk(False, xf_ref, bf_ref, cf_ref, dtf_ref, alog_ref[0], e_ref, yf_ref, h_ref.at[0])
    _scan_chunk(True, xb_ref, bb_ref, cb_ref, dtb_ref, alog_ref[1], e_ref, yb_ref, h_ref.at[1])


def _ssd_scan_call(geom, xbc, dt, a_log, e_mat):
    rows = geom.rows
    nc_lat = geom.seq // CHUNK
    nc_ctx = geom.ctx // CHUNK
    lat_blocks = geom.n_lat // CHUNK

    def chunk_block(d, b, s):
        in_ctx = s < nc_ctx
        sl = s - nc_ctx
        c_ctx = nc_ctx - 1 - s if d else s
        c_lat = nc_lat - 1 - sl if d else sl
        return jnp.where(in_ctx, lat_blocks + b * nc_ctx + c_ctx, b * nc_lat + c_lat)

    xcols = D_INNER // D_BC

    def dir_specs(d):
        return [
            pl.BlockSpec((CHUNK, D_INNER), lambda b, s: (chunk_block(d, b, s), 0)),
            pl.BlockSpec((CHUNK, D_BC), lambda b, s: (chunk_block(d, b, s), xcols)),
            pl.BlockSpec((CHUNK, D_BC), lambda b, s: (chunk_block(d, b, s), xcols + 1)),
            pl.BlockSpec((None, CHUNK, LANES), lambda b, s: (d, chunk_block(d, b, s), 0)),
        ]

    y_shape = jax.ShapeDtypeStruct((rows, D_INNER), BF16)
    return pl.pallas_call(
        _ssd_scan_kernel,
        grid=(geom.bsz, nc_ctx + nc_lat),
        in_specs=dir_specs(0) + dir_specs(1) + [
            pl.BlockSpec((2, 1, SSD_HEADS), lambda b, s: (0, 0, 0)),
            pl.BlockSpec((2 * SSD_HEADS, D_INNER), lambda b, s: (0, 0)),
        ],
        out_specs=[pl.BlockSpec((CHUNK, D_INNER), lambda b, s: (chunk_block(0, b, s), 0)),
                   pl.BlockSpec((CHUNK, D_INNER), lambda b, s: (chunk_block(1, b, s), 0))],
        out_shape=[y_shape, y_shape],
        scratch_shapes=[pltpu.VMEM((2, D_STATE, D_INNER), F32)],
        compiler_params=_cparams(("arbitrary", "arbitrary")),
        name="ssd_scan",
    )(xbc, xbc, xbc, dt, xbc, xbc, xbc, dt, a_log, e_mat)


def _ssd_out_kernel(yf_ref, yb_ref, xs_ref, z_ref, dsk_ref, gw_ref, w_ref, x_ref, gate_ref, o_ref):
    y = (yf_ref[...].astype(F32) + yb_ref[...].astype(F32)
         + xs_ref[...].astype(F32) * dsk_ref[...])
    z = z_ref[...].astype(F32)
    yn = _rms(y * (z * _sigmoid(z)), gw_ref[...]).astype(BF16)
    o_ref[...] = x_ref[...] + gate_ref[...] * _dot(yn, w_ref[...])


def _ssd_out_call(geom, layer, y_f, y_b, xbc, zx, d_skip_e, gnorm_w, w_out, x_all, mod3):
    tm = min(geom.tm, 512)
    rows = geom.rows
    return pl.pallas_call(
        _ssd_out_kernel,
        grid=(rows // tm,),
        in_specs=[
            pl.BlockSpec((tm, D_INNER), lambda i: (i, 0)),
            pl.BlockSpec((tm, D_INNER), lambda i: (i, 0)),
            pl.BlockSpec((tm, D_INNER), lambda i: (i, 0)),
            pl.BlockSpec((tm, D_INNER), lambda i: (i, 0)),
            pl.BlockSpec((1, D_INNER), lambda i: (0, 0)),
            pl.BlockSpec((1, D_INNER), lambda i: (0, 0)),
            _resident((D_INNER, D_MODEL), lambda i: (0, 0)),
            pl.BlockSpec((tm, D_MODEL), lambda i: (i, 0)),
            geom.mod_spec(layer, 2, tm),
        ],
        out_specs=pl.BlockSpec((tm, D_MODEL), lambda i: (i, 0)),
        out_shape=jax.ShapeDtypeStruct((rows, D_MODEL), F32),
        compiler_params=_cparams(("arbitrary",)),
        name="ssd_out_proj",
    )(y_f, y_b, xbc, zx, d_skip_e, gnorm_w, w_out, x_all, mod3)


def _rope_blocks(acc, cos, sin):
    outs = []
    for c in range(acc.shape[1] // LANES):
        xb = acc[:, c * LANES:(c + 1) * LANES]
        outs.append(xb * cos + pltpu.roll(xb, LANES // 2, 1) * sin)
    return outs


def _qkv_kernel(x_ref, nw_ref, sh_ref, sc_ref, w_ref, cos_ref, sin_ref,
                q1_ref, q2_ref, k_ref, v_ref, a_scr):
    j = pl.program_id(1)

    @pl.when(j == 0)
    def _():
        a = _rms(x_ref[...], nw_ref[...]) * (1.0 + sc_ref[...]) + sh_ref[...]
        a_scr[...] = a.astype(BF16)

    acc = _dot(a_scr[...], w_ref[...])

    @pl.when(j == 0)
    def _():
        scale = DA_HEAD_DIM ** -0.5 * math.log2(math.e)
        lane = lax.broadcasted_iota(jnp.int32, (acc.shape[0], LANES), 1)
        is_map1 = (lane % (LANES // 2)) < (LANES // 4)
        for c, blk in enumerate(_rope_blocks(acc, cos_ref[...], sin_ref[...])):
            blk = blk * scale
            q1_ref[:, c * LANES:(c + 1) * LANES] = jnp.where(is_map1, blk, 0.0).astype(BF16)
            q2_ref[:, c * LANES:(c + 1) * LANES] = jnp.where(is_map1, 0.0, blk).astype(BF16)

    @pl.when(j == 1)
    def _():
        for c, blk in enumerate(_rope_blocks(acc, cos_ref[...], sin_ref[...])):
            k_ref[:, c * LANES:(c + 1) * LANES] = blk.astype(BF16)

    @pl.when(j == 2)
    def _():
        v_ref[...] = acc.astype(BF16)


def _qkv_call(geom, layer, x_all, mod3, norm_w, w_qkv, cos_t, sin_t):
    tm = geom.tm
    rows = geom.rows
    row_out = pl.BlockSpec((tm, D_MODEL), lambda i, j: (i, 0))
    return pl.pallas_call(
        _qkv_kernel,
        grid=(rows // tm, 3),
        in_specs=[
            pl.BlockSpec((tm, D_MODEL), lambda i, j: (i, 0)),
            pl.BlockSpec((1, D_MODEL), lambda i, j: (0, 0)),
            geom.mod_spec(layer, 0, tm),
            geom.mod_spec(layer, 1, tm),
            pl.BlockSpec((D_MODEL, D_MODEL), lambda i, j: (0, j)),
            pl.BlockSpec((tm, LANES), lambda i, j: (i, 0)),
            pl.BlockSpec((tm, LANES), lambda i, j: (i, 0)),
        ],
        out_specs=[row_out, row_out, row_out, row_out],
        out_shape=[jax.ShapeDtypeStruct((rows, D_MODEL), BF16)] * 4,
        scratch_shapes=[pltpu.VMEM((tm, D_MODEL), BF16)],
        compiler_params=_cparams(("arbitrary", "arbitrary")),
        name="attn_qkv_rope",
    )(x_all, norm_w, mod3, mod3, w_qkv, cos_t, sin_t)


FLASH_ROWS = 64


def _flash_kernel(lam_ref, subw_ref, q1_ref, q2_ref, *refs, seg_lens, tk, lambda_init):
    n_seg = len(seg_lens)
    kv_refs = refs[:2 * n_seg]
    o_ref, s_scr, p_scr, acc_scr, m_scr, al_scr = refs[2 * n_seg + 1:]
    tq = q1_ref.shape[0]
    qs = (q1_ref[...], q2_ref[...])
    acc_scr[...] = jnp.zeros_like(acc_scr)
    m_scr[...] = jnp.full_like(m_scr, -jnp.inf)

    def scores(slot, k_blk):
        w = k_blk.shape[0]
        for t in range(2):
            s_scr[slot, t, :, :w] = _dot_nt(qs[t], k_blk)

    def absorb(slot, v_blk):
        w = v_blk.shape[0]
        vaug = jnp.concatenate([v_blk, jnp.ones_like(v_blk)], axis=1)
        for t in range(2):
            for r in range(0, tq, FLASH_ROWS):
                rows = pl.ds(r, FLASH_ROWS)
                tiles = [s_scr[slot, t, rows, c:c + LANES] for c in range(0, w, LANES)]
                mx = tiles[0]
                for tl in tiles[1:]:
                    mx = jnp.maximum(mx, tl)
                m_old = m_scr[t, rows, :]
                m_new = jnp.maximum(m_old, jnp.max(mx, axis=1, keepdims=True))
                for ci, tl in enumerate(tiles):
                    p_scr[t, rows, ci * LANES:(ci + 1) * LANES] = jnp.exp2(tl - m_new).astype(BF16)
                m_scr[t, rows, :] = m_new
                al_scr[t, rows, :] = jnp.exp2(m_old - m_new)
            alpha = al_scr[t]
            acc_scr[t] = (jnp.concatenate([alpha, alpha], axis=1) * acc_scr[t]
                          + _dot(p_scr[t, :, :w], vaug))

    if n_seg == 1:
        scores(0, kv_refs[0][...])
        absorb(0, kv_refs[1][...])
    else:
        kl_ref, vl_ref, kc_ref, vc_ref = kv_refs
        n_lat = seg_lens[0] // tk

        def k_block(i):
            return kc_ref[...] if i == n_lat else kl_ref[i * tk:(i + 1) * tk, :]

        def v_block(i):
            return vc_ref[...] if i == n_lat else vl_ref[i * tk:(i + 1) * tk, :]

        def step(i, slot):
            if i + 1 <= n_lat:
                scores(1 - slot, k_block(i + 1))
            absorb(slot, v_block(i))

        scores(0, k_block(0))
        for i in range(n_lat + 1):
            step(i, i % 2)

    lam_v = lam_ref[...]
    hd = DA_HEAD_DIM
    lam = (jnp.exp(jnp.sum(lam_v[:, 0:hd] * lam_v[:, hd:2 * hd], axis=1, keepdims=True))
           - jnp.exp(jnp.sum(lam_v[:, 2 * hd:3 * hd] * lam_v[:, 3 * hd:4 * hd], axis=1, keepdims=True))
           + lambda_init)
    a1 = acc_scr[0]
    a2 = acc_scr[1]
    o = a1[:, :LANES] / a1[:, LANES:] - lam * (a2[:, :LANES] / a2[:, LANES:])
    o = _rms(o, subw_ref[...]) * (1.0 - lambda_init)
    o_ref[...] = o.astype(BF16)


def _flash_call(geom, q1, q2, k, v, lam_vec, subln_w, lambda_init, o_prev, ctx_queries):
    bsz, seq, ctx = geom.bsz, geom.seq, geom.ctx
    tk = 512
    if ctx_queries:
        tq = min(256, ctx)
        nq = ctx // tq
        q_base = geom.n_lat // tq
        seg_lens = (ctx,)
    else:
        tq = min(256, seq)
        nq = seq // tq
        q_base = 0
        seg_lens = (seq, ctx)
    ctx_base = geom.n_lat // ctx

    def q_map(b, h, i):
        return (q_base + b * nq + i, h)

    kv_lat = pl.BlockSpec((seq, LANES), lambda b, h, i: (b, h))
    kv_ctx = pl.BlockSpec((ctx, LANES), lambda b, h, i: (ctx_base + b, h))
    kv_specs = [kv_ctx, kv_ctx] if ctx_queries else [kv_lat, kv_lat, kv_ctx, kv_ctx]
    kv_args = [k, v] if ctx_queries else [k, v, k, v]
    kern = functools.partial(_flash_kernel, seg_lens=seg_lens, tk=tk, lambda_init=lambda_init)
    n_in = 4 + len(kv_args)
    return pl.pallas_call(
        kern,
        grid=(bsz, DA_HEADS, nq),
        in_specs=[
            pl.BlockSpec((1, 4 * DA_HEAD_DIM), lambda b, h, i: (0, 0)),
            pl.BlockSpec((1, LANES), lambda b, h, i: (0, 0)),
            pl.BlockSpec((tq, LANES), q_map),
            pl.BlockSpec((tq, LANES), q_map),
            *kv_specs,
            pl.BlockSpec(memory_space=pl.ANY),
        ],
        out_specs=pl.BlockSpec((tq, LANES), q_map),
        out_shape=jax.ShapeDtypeStruct((geom.rows, D_MODEL), BF16),
        input_output_aliases={n_in: 0},
        scratch_shapes=[pltpu.VMEM((2, 2, tq, tk), F32),
                        pltpu.VMEM((2, tq, tk), BF16),
                        pltpu.VMEM((2, tq, 2 * LANES), F32),
                        pltpu.VMEM((2, tq, LANES), F32),
                        pltpu.VMEM((2, tq, LANES), F32)],
        compiler_params=_cparams(("arbitrary", "arbitrary", "arbitrary")),
        name="flash_ctx" if ctx_queries else "flash_lat",
    )(lam_vec, subln_w, q1, q2, *kv_args, o_prev)


def _attn_out_kernel(o_ref, w_ref, x_ref, gate_ref, out_ref):
    out_ref[...] = x_ref[...] + gate_ref[...] * _dot(o_ref[...], w_ref[...])


def _attn_out_call(geom, layer, o, w_o, x_all, mod3):
    tm = geom.tm
    rows = geom.rows
    return pl.pallas_call(
        _attn_out_kernel,
        grid=(rows // tm,),
        in_specs=[
            pl.BlockSpec((tm, D_MODEL), lambda i: (i, 0)),
            _resident((D_MODEL, D_MODEL), lambda i: (0, 0)),
            pl.BlockSpec((tm, D_MODEL), lambda i: (i, 0)),
            geom.mod_spec(layer, 2, tm),
        ],
        out_specs=pl.BlockSpec((tm, D_MODEL), lambda i: (i, 0)),
        out_shape=jax.ShapeDtypeStruct((rows, D_MODEL), F32),
        compiler_params=_cparams(("arbitrary",)),
        name="attn_out_proj",
    )(o, w_o, x_all, mod3)


FFN_TILE = 256


def _ffn_kernel(x_ref, nw_ref, sh_ref, sc_ref, gate_ref, wg_ref, wu_ref, wd_ref, fw_ref,
                o_ref, h_scr, *, final_norm):
    x = x_ref[...]
    a = (_rms(x, nw_ref[...]) * (1.0 + sc_ref[...]) + sh_ref[...]).astype(BF16)
    for f in range(0, D_FF, FFN_TILE):
        g = _dot(a, wg_ref[:, f:f + FFN_TILE])
        u = _dot(a, wu_ref[:, f:f + FFN_TILE])
        h_scr[:, f:f + FFN_TILE] = (g * _sigmoid(g) * u).astype(BF16)
    out = x + gate_ref[...] * _dot(h_scr[...], wd_ref[...])
    if final_norm:
        out = _rms(out, fw_ref[...])
    o_ref[...] = out


def _ffn_call(geom, layer, x_all, mod3, norm_w, w_gate, w_up, w_down, final_w, final_norm):
    tm = min(geom.tm, 512)
    rows = geom.n_lat if final_norm else geom.rows
    kern = functools.partial(_ffn_kernel, final_norm=final_norm)
    return pl.pallas_call(
        kern,
        grid=(rows // tm,),
        in_specs=[
            pl.BlockSpec((tm, D_MODEL), lambda i: (i, 0)),
            pl.BlockSpec((1, D_MODEL), lambda i: (0, 0)),
            geom.mod_spec(layer, 3, tm),
            geom.mod_spec(layer, 4, tm),
            geom.mod_spec(layer, 5, tm),
            _resident((D_MODEL, D_FF), lambda i: (0, 0)),
            _resident((D_MODEL, D_FF), lambda i: (0, 0)),
            _resident((D_FF, D_MODEL), lambda i: (0, 0)),
            pl.BlockSpec((1, D_MODEL), lambda i: (0, 0)),
        ],
        out_specs=pl.BlockSpec((tm, D_MODEL), lambda i: (i, 0)),
        out_shape=jax.ShapeDtypeStruct((rows, D_MODEL), F32),
        scratch_shapes=[pltpu.VMEM((tm, D_FF), BF16)],
        compiler_params=_cparams(("arbitrary",)),
        name="ffn_swiglu",
    )(x_all, norm_w, mod3, mod3, mod3, w_gate, w_up, w_down, final_w)


def _rope_tables(geom):
    seq = geom.seq
    t = jnp.arange(seq)
    row = (t // GRID_W).astype(F32)
    col = (t % GRID_W).astype(F32)
    freqs = 1.0 / (ROPE_THETA ** (jnp.arange(ROPE_PAIRS, dtype=F32) / ROPE_PAIRS))
    ang = jnp.concatenate([row[:, None] * freqs, col[:, None] * freqs], axis=-1)
    cos, sin = jnp.cos(ang), jnp.sin(ang)
    cos_l = jnp.tile(cos, (geom.bsz, 4))
    sin_l = jnp.tile(jnp.concatenate([-sin, -sin, sin, sin], axis=-1), (geom.bsz, 1))
    cos_t = jnp.concatenate([cos_l, jnp.ones((geom.n_ctx, LANES), F32)], axis=0)
    sin_t = jnp.concatenate([sin_l, jnp.zeros((geom.n_ctx, LANES), F32)], axis=0)
    return cos_t, sin_t


def _qk_col_perm():
    perm = []
    quarter = DA_HEAD_DIM // 2
    for h in range(DA_HEADS):
        for half in range(2):
            for m in range(2):
                base = h * 2 * DA_HEAD_DIM + m * DA_HEAD_DIM + half * quarter
                perm.extend(range(base, base + quarter))
    return jnp.asarray(perm, dtype=jnp.int32)


def kernel(x, c, ctx, c_ctx, w_ada, b_ada, norm1_w, norm2_w, ssd_w_in, ssd_conv_w, ssd_conv_b,
           ssd_a_log, ssd_dt_bias, ssd_d, ssd_norm_w, ssd_w_out, da_w_qkv, da_w_o, da_lq1, da_lk1,
           da_lq2, da_lk2, da_subln_w, ffn_w_gate, ffn_w_up, ffn_w_down, final_norm_w):
    bsz, seq, d = x.shape
    n_ctx_tok = ctx.shape[1]
    assert d == D_MODEL and bsz + 1 <= MOD_ROWS
    assert seq % CONV_ROWS == 0 and n_ctx_tok % CONV_ROWS == 0 and seq % GRID_W == 0
    geom = _Geom(bsz, seq, n_ctx_tok)

    x_all = jnp.concatenate([x.reshape(bsz * seq, d), ctx.reshape(bsz * n_ctx_tok, d)], axis=0)
    cvec = jnp.concatenate([c, c_ctx[None, :], jnp.zeros((MOD_ROWS - bsz - 1, d), F32)], axis=0)
    mod = _ada_call(cvec, w_ada.astype(BF16), b_ada)
    mod3 = mod.reshape(DEPTH * MOD_ROWS * 6, 1, D_MODEL)

    cos_t, sin_t = _rope_tables(geom)
    perm = _qk_col_perm()
    hq = DA_HEADS * 2 * DA_HEAD_DIM
    e_mat = (jnp.arange(D_INNER)[None, :] // SSD_HEAD_DIM == jnp.arange(SSD_HEADS)[:, None]).astype(BF16)
    e_mat = jnp.concatenate([e_mat, e_mat], axis=0)

    out = None
    for i in range(DEPTH):
        j = i // 2
        last = i == DEPTH - 1
        n1 = norm1_w[i].reshape(1, d)
        if i % 2 == 0:
            w_in = ssd_w_in[j]
            w_zx = w_in[:, :D_ZX].astype(BF16)
            w_dt = w_in[:, D_ZX:].reshape(d, 2, SSD_HEADS).transpose(1, 0, 2)
            w_dt = jnp.pad(w_dt, ((0, 0), (0, 0), (0, LANES - SSD_HEADS))).astype(BF16)
            dt_b = jnp.pad(ssd_dt_bias[j], ((0, 0), (0, LANES - SSD_HEADS))).reshape(2, 1, LANES)
            zx, dt = _ssd_in_call(geom, i, x_all, mod3, n1, w_zx, w_dt, dt_b)
            xbc = _conv_call(geom, zx, ssd_conv_w[j], ssd_conv_b[j].reshape(1, D_CONV_CH))
            y_f, y_b = _ssd_scan_call(geom, xbc, dt, ssd_a_log[j].reshape(2, 1, SSD_HEADS), e_mat)
            d_skip_e = jnp.repeat(ssd_d[j], SSD_HEAD_DIM).reshape(1, D_INNER)
            x_all = _ssd_out_call(geom, i, y_f, y_b, xbc, zx, d_skip_e, ssd_norm_w[j].reshape(1, D_INNER),
                                  ssd_w_out[j].astype(BF16), x_all, mod3)
        else:
            lambda_init = 0.8 - 0.6 * math.exp(-0.3 * i)
            w_qkv = da_w_qkv[j]
            w_qkv = jnp.concatenate([w_qkv[:, :hq][:, perm], w_qkv[:, hq:2 * hq][:, perm],
                                     w_qkv[:, 2 * hq:]], axis=1).astype(BF16)
            q1, q2, k, v = _qkv_call(geom, i, x_all, mod3, n1, w_qkv, cos_t, sin_t)
            lam_vec = jnp.concatenate([da_lq1[j], da_lk1[j], da_lq2[j], da_lk2[j]]).reshape(1, 4 * DA_HEAD_DIM)
            subw = da_subln_w[j].reshape(1, LANES)
            o = jnp.zeros((geom.rows, D_MODEL), BF16)
            o = _flash_call(geom, q1, q2, k, v, lam_vec, subw, lambda_init, o, ctx_queries=False)
            if not last:
                o = _flash_call(geom, q1, q2, k, v, lam_vec, subw, lambda_init, o, ctx_queries=True)
            x_all = _attn_out_call(geom, i, o, da_w_o[j].astype(BF16), x_all, mod3)
        res = _ffn_call(geom, i, x_all, mod3, norm2_w[i].reshape(1, d), ffn_w_gate[i].astype(BF16),
                        ffn_w_up[i].astype(BF16), ffn_w_down[i].astype(BF16),
                        final_norm_w.reshape(1, d), final_norm=last)
        if last:
            out = res
        else:
            x_all = res
    return out.reshape(bsz, seq, d)
```

```python
import functools
import math

import jax
import jax.numpy as jnp
from jax import lax
from jax.experimental import pallas as pl
from jax.experimental.pallas import tpu as pltpu

F32 = jnp.float32
BF16 = jnp.bfloat16

D_MODEL = 1024
DEPTH = 4
GRID_W = 64
EPS = 1e-6

D_INNER = 2 * D_MODEL
SSD_HEAD_DIM = 64
SSD_HEADS = D_INNER // SSD_HEAD_DIM
SSD_GROUPS = 4
HEADS_PER_GROUP = SSD_HEADS // SSD_GROUPS
D_STATE = 128
CONV_W = 5
CHUNK = 128
D_BC = SSD_GROUPS * D_STATE
D_CONV_CH = D_INNER + 2 * D_BC
D_ZX = D_INNER + D_CONV_CH

DA_HEADS = D_MODEL // 128
DA_HEAD_DIM = 64
ROPE_THETA = 10000.0
ROPE_PAIRS = DA_HEAD_DIM // 4

D_FF = -(-8 * D_MODEL // (3 * 256)) * 256

LANES = 128
LOG2E = math.log2(math.e)
MOD_ROWS = 16
VMEM_LIMIT = 56 * 1024 * 1024


def _cparams(sem):
    return pltpu.CompilerParams(dimension_semantics=sem, vmem_limit_bytes=VMEM_LIMIT)


def _dot(a, b):
    return jnp.dot(a, b, preferred_element_type=F32)


def _dot_nt(a, b):
    return lax.dot_general(a, b, (((1,), (1,)), ((), ())), preferred_element_type=F32)


def _sigmoid(x):
    return 1.0 / (1.0 + jnp.exp(-x))


def _rms(xf, w):
    return xf * lax.rsqrt(jnp.mean(xf * xf, axis=-1, keepdims=True) + EPS) * w


def _resident(shape, index_map):
    return pl.BlockSpec(shape, index_map, pipeline_mode=pl.Buffered(1))


def _ada_kernel(c_ref, w_ref, b_ref, o_ref):
    cv = c_ref[...]
    s = (cv * _sigmoid(cv)).astype(BF16)
    o_ref[...] = _dot(s, w_ref[...]) + b_ref[...]


def _ada_call(cvec, w_ada, b_ada):
    depth, d, n = w_ada.shape
    tn = 2048
    return pl.pallas_call(
        _ada_kernel,
        grid=(depth, n // tn),
        in_specs=[
            pl.BlockSpec((MOD_ROWS, d), lambda l, j: (0, 0)),
            pl.BlockSpec((None, d, tn), lambda l, j: (l, 0, j)),
            pl.BlockSpec((None, 1, tn), lambda l, j: (l, 0, j)),
        ],
        out_specs=pl.BlockSpec((None, MOD_ROWS, tn), lambda l, j: (l, 0, j)),
        out_shape=jax.ShapeDtypeStruct((depth, MOD_ROWS, n), F32),
        compiler_params=_cparams(("arbitrary", "arbitrary")),
        name="adaln",
    )(cvec, w_ada, b_ada.reshape(depth, 1, n))


class _Geom:
    def __init__(self, bsz, seq, ctx):
        self.bsz, self.seq, self.ctx = bsz, seq, ctx
        self.n_lat = bsz * seq
        self.n_ctx = bsz * ctx
        self.rows = self.n_lat + self.n_ctx
        tm = 1024
        while seq % tm or self.n_ctx % tm:
            tm //= 2
        self.tm = tm

    def mod_spec(self, layer, k, tm):
        n_lat_tiles = self.n_lat // tm
        per_batch = self.seq // tm
        bsz = self.bsz

        def imap(i, *_):
            row = jnp.where(i < n_lat_tiles, i // per_batch, bsz)
            return ((layer * MOD_ROWS + row) * 6 + k, 0, 0)

        return pl.BlockSpec((None, 1, D_MODEL), imap)


def _softplus(x):
    return jnp.maximum(x, 0.0) + jnp.log1p(jnp.exp(-jnp.abs(x)))


PROJ_TILE = 512


def _split3(v):
    p1 = v.astype(BF16)
    r1 = v - p1.astype(F32)
    p2 = r1.astype(BF16)
    p3 = (r1 - p2.astype(F32)).astype(BF16)
    return p1, p2, p3


def _scan_mask(backward):
    li = lax.broadcasted_iota(jnp.int32, (CHUNK, CHUNK), 0)
    si = lax.broadcasted_iota(jnp.int32, (CHUNK, CHUNK), 1)
    return si >= li if backward else si <= li


def _scan_terms(backward, dt, alog):
    nh = SSD_HEADS
    a = -jnp.exp(alog)
    adt = dt * (a * LOG2E)
    mask = _scan_mask(backward)
    cum3 = _dot(mask.astype(F32).astype(BF16), jnp.concatenate(_split3(adt), axis=1))
    cum = cum3[:, :nh] + cum3[:, nh:2 * nh] + cum3[:, 2 * nh:3 * nh]
    tot = jnp.sum(adt, axis=0, keepdims=True)
    pad = jnp.zeros((CHUNK, CHUNK - nh), F32)
    tr = jnp.concatenate([cum - jnp.log2(dt), pad], axis=1).T[:nh]
    packed = jnp.concatenate([cum, dt * jnp.exp2(tot - cum), jnp.exp2(cum),
                              jnp.broadcast_to(jnp.exp2(tot), (CHUNK, nh))], axis=1)
    return packed, tr


CONV_ROWS = 256
HALO = 16


def _ssd_in_kernel(xp_ref, x_ref, xn_ref, nw_ref, sh_ref, sc_ref, w_ref, wdt_ref, dtb_ref, alog_ref,
                   cw_ref, cb_ref, z_ref, xbc_ref, pk_ref, tr_ref, *, n_lat, seq, ctx):
    def norm_mod(v):
        return (_rms(v, nw_ref[...]) * (1.0 + sc_ref[...]) + sh_ref[...]).astype(BF16)

    ab = norm_mod(x_ref[...])
    tm = ab.shape[0]
    a_ext = jnp.concatenate([norm_mod(xp_ref[...]), ab, norm_mod(xn_ref[...])], axis=0)
    for d in range(2):
        dt = _softplus(_dot(ab, wdt_ref[d]) + dtb_ref[d])
        for c in range(tm // CHUNK):
            packed, tr = _scan_terms(d == 1, dt[c * CHUNK:(c + 1) * CHUNK, :SSD_HEADS], alog_ref[d])
            pk_ref[d, c * CHUNK:(c + 1) * CHUNK, :] = packed
            tr_ref[d, c * SSD_HEADS:(c + 1) * SSD_HEADS, :] = tr
    for c in range(0, D_INNER, PROJ_TILE):
        z_ref[:, c:c + PROJ_TILE] = _dot(ab, w_ref[:, c:c + PROJ_TILE]).astype(BF16)

    row0 = pl.program_id(0) * tm
    keep = []
    for s in range(tm // CONV_ROWS):
        g0 = row0 + s * CONV_ROWS
        g1 = g0 + CONV_ROWS
        starts = jnp.where(g0 < n_lat, g0 % seq == 0, (g0 - n_lat) % ctx == 0)
        ends = jnp.where(g1 <= n_lat, g1 % seq == 0, (g1 - n_lat) % ctx == 0)
        keep.append((jnp.where(starts, 0.0, 1.0), jnp.where(ends, 0.0, 1.0)))
    n_ext = CONV_ROWS + 16
    mid = CONV_W // 2
    for c in range(0, D_CONV_CH, PROJ_TILE):
        acc = _dot(a_ext, w_ref[:, D_INNER + c:D_INNER + c + PROJ_TILE])
        cw = cw_ref[:, c:c + PROJ_TILE]
        cbias = cb_ref[:, c:c + PROJ_TILE]
        for s in range(tm // CONV_ROWS):
            base = HALO + s * CONV_ROWS
            ext = jnp.concatenate([acc[base - 8:base] * keep[s][0], acc[base:base + CONV_ROWS],
                                   acc[base + CONV_ROWS:base + CONV_ROWS + 8] * keep[s][1]], axis=0)
            y = cbias + cw[mid:mid + 1, :] * ext[8:8 + CONV_ROWS]
            for k in range(CONV_W):
                if k != mid:
                    rolled = pltpu.roll(ext, (mid - k) % n_ext, 0)
                    y = y + cw[k:k + 1, :] * rolled[8:8 + CONV_ROWS]
            xbc_ref[s * CONV_ROWS:(s + 1) * CONV_ROWS, c:c + PROJ_TILE] = (y * _sigmoid(y)).astype(BF16)


def _ssd_in_call(geom, layer, x_all, mod3, norm_w, w_zx, w_dt, dt_bias, a_log, conv_w, conv_b):
    tm = min(geom.tm, 512)
    rows = geom.rows
    tr_rows = tm // CHUNK * SSD_HEADS
    hb = tm // HALO
    n_halo = rows // HALO
    kern = functools.partial(_ssd_in_kernel, n_lat=geom.n_lat, seq=geom.seq, ctx=geom.ctx)
    return pl.pallas_call(
        kern,
        grid=(rows // tm,),
        in_specs=[
            pl.BlockSpec((HALO, D_MODEL), lambda i: (jnp.maximum(i * hb - 1, 0), 0)),
            pl.BlockSpec((tm, D_MODEL), lambda i: (i, 0)),
            pl.BlockSpec((HALO, D_MODEL), lambda i: (jnp.minimum((i + 1) * hb, n_halo - 1), 0)),
            pl.BlockSpec((1, D_MODEL), lambda i: (0, 0)),
            geom.mod_spec(layer, 0, tm),
            geom.mod_spec(layer, 1, tm),
            _resident((D_MODEL, D_ZX), lambda i: (0, 0)),
            pl.BlockSpec((2, D_MODEL, LANES), lambda i: (0, 0, 0)),
            pl.BlockSpec((2, 1, LANES), lambda i: (0, 0, 0)),
            pl.BlockSpec((2, 1, SSD_HEADS), lambda i: (0, 0, 0)),
            pl.BlockSpec((CONV_W, D_CONV_CH), lambda i: (0, 0)),
            pl.BlockSpec((1, D_CONV_CH), lambda i: (0, 0)),
        ],
        out_specs=[
            pl.BlockSpec((tm, D_INNER), lambda i: (i, 0)),
            pl.BlockSpec((tm, D_CONV_CH), lambda i: (i, 0)),
            pl.BlockSpec((2, tm, LANES), lambda i: (0, i, 0)),
            pl.BlockSpec((2, tr_rows, LANES), lambda i: (0, i, 0)),
        ],
        out_shape=[
            jax.ShapeDtypeStruct((rows, D_INNER), BF16),
            jax.ShapeDtypeStruct((rows, D_CONV_CH), BF16),
            jax.ShapeDtypeStruct((2, rows, LANES), F32),
            jax.ShapeDtypeStruct((2, rows // CHUNK * SSD_HEADS, LANES), F32),
        ],
        compiler_params=_cparams(("arbitrary",)),
        name="ssd_in_proj_conv",
    )(x_all, x_all, x_all, norm_w, mod3, mod3, w_zx, w_dt, dt_bias, a_log, conv_w, conv_b)


def _expand_heads(v, e2_ref):
    hi = v.astype(BF16)
    lo = (v - hi.astype(F32)).astype(BF16)
    return _dot(jnp.concatenate([hi, lo], axis=1), e2_ref[...])


def _scan_prep(backward, pk_ref, tr_ref, e_ref):
    nh = SSD_HEADS
    pk = pk_ref[...]
    cum = pk[:, :nh]
    ex = _expand_heads(jnp.concatenate([pk[:, nh:2 * nh], pk[:, 2 * nh:3 * nh], pk[:8, 3 * nh:]], axis=0),
                       e_ref)
    return (_scan_mask(backward), cum, tr_ref[...], ex[:CHUNK], ex[CHUNK:2 * CHUNK],
            ex[2 * CHUNK:2 * CHUNK + 1])


def _scan_group(g, prep, xs_ref, b_ref, c_ref, y_ref, h_ref):
    mask, cum, tr, wst_e, dout_e, cdec_e = prep
    lane = lax.broadcasted_iota(jnp.int32, (CHUNK, 2 * SSD_HEAD_DIM), 1)
    gw = HEADS_PER_GROUP * SSD_HEAD_DIM
    bg = b_ref[:, g * D_STATE:(g + 1) * D_STATE]
    cg = c_ref[:, g * D_STATE:(g + 1) * D_STATE]
    cb = _dot_nt(cg, bg)
    h_old = h_ref[:, g * gw:(g + 1) * gw]
    y_off = _dot(cg, h_old.astype(BF16)) * dout_e[:, g * gw:(g + 1) * gw]
    xg = xs_ref[:, g * gw:(g + 1) * gw]
    for q in range(HEADS_PER_GROUP // 2):
        ws = []
        for hh in range(2):
            h = g * HEADS_PER_GROUP + 2 * q + hh
            diff = cum[:, h:h + 1] - tr[h:h + 1, :]
            lmat = jnp.exp2(jnp.where(mask, diff, -jnp.inf))
            ws.append((cb * lmat).astype(BF16))
        xp = xg[:, q * 2 * SSD_HEAD_DIM:(q + 1) * 2 * SSD_HEAD_DIM]
        zero = jnp.zeros_like(xp)
        rhs = jnp.concatenate([jnp.where(lane < SSD_HEAD_DIM, xp, zero),
                               jnp.where(lane >= SSD_HEAD_DIM, xp, zero)], axis=0)
        yd = _dot(jnp.concatenate(ws, axis=1), rhs)
        c0 = g * gw + q * 2 * SSD_HEAD_DIM
        y_ref[:, c0:c0 + 2 * SSD_HEAD_DIM] = (
            yd + y_off[:, q * 2 * SSD_HEAD_DIM:(q + 1) * 2 * SSD_HEAD_DIM]).astype(y_ref.dtype)
    xw = (xg.astype(F32) * wst_e[:, g * gw:(g + 1) * gw]).astype(BF16)
    bgt = bg.astype(F32).T.astype(BF16)
    h_ref[:, g * gw:(g + 1) * gw] = h_old * cdec_e[:, g * gw:(g + 1) * gw] + _dot(bgt, xw)


def _ssd_scan_kernel(xf_ref, bf_ref, cf_ref, pkf_ref, trf_ref, xb_ref, bb_ref, cb_ref, pkb_ref, trb_ref,
                     e_ref, yf_ref, yb_ref, h_ref):
    @pl.when(pl.program_id(1) == 0)
    def _():
        h_ref[...] = jnp.zeros_like(h_ref)

    prep_f = _scan_prep(False, pkf_ref, trf_ref, e_ref)
    prep_b = _scan_prep(True, pkb_ref, trb_ref, e_ref)
    for g in range(SSD_GROUPS):
        _scan_group(g, prep_f, xf_ref, bf_ref, cf_ref, yf_ref, h_ref.at[0])
        _scan_group(g, prep_b, xb_ref, bb_ref, cb_ref, yb_ref, h_ref.at[1])


def _ssd_scan_call(geom, xbc, packed, tr, e_mat):
    rows = geom.rows
    nc_lat = geom.seq // CHUNK
    nc_ctx = geom.ctx // CHUNK
    lat_blocks = geom.n_lat // CHUNK

    def chunk_block(d, b, s):
        in_ctx = s < nc_ctx
        sl = s - nc_ctx
        c_ctx = nc_ctx - 1 - s if d else s
        c_lat = nc_lat - 1 - sl if d else sl
        return jnp.where(in_ctx, lat_blocks + b * nc_ctx + c_ctx, b * nc_lat + c_lat)

    xcols = D_INNER // D_BC

    def dir_specs(d):
        return [
            pl.BlockSpec((CHUNK, D_INNER), lambda b, s: (chunk_block(d, b, s), 0)),
            pl.BlockSpec((CHUNK, D_BC), lambda b, s: (chunk_block(d, b, s), xcols)),
            pl.BlockSpec((CHUNK, D_BC), lambda b, s: (chunk_block(d, b, s), xcols + 1)),
            pl.BlockSpec((None, CHUNK, LANES), lambda b, s: (d, chunk_block(d, b, s), 0)),
            pl.BlockSpec((None, SSD_HEADS, LANES), lambda b, s: (d, chunk_block(d, b, s), 0)),
        ]

    y_shape = jax.ShapeDtypeStruct((rows, D_INNER), BF16)
    return pl.pallas_call(
        _ssd_scan_kernel,
        grid=(geom.bsz, nc_ctx + nc_lat),
        in_specs=dir_specs(0) + dir_specs(1) + [
            pl.BlockSpec((2 * SSD_HEADS, D_INNER), lambda b, s: (0, 0)),
        ],
        out_specs=[pl.BlockSpec((CHUNK, D_INNER), lambda b, s: (chunk_block(0, b, s), 0)),
                   pl.BlockSpec((CHUNK, D_INNER), lambda b, s: (chunk_block(1, b, s), 0))],
        out_shape=[y_shape, y_shape],
        scratch_shapes=[pltpu.VMEM((2, D_STATE, D_INNER), F32)],
        compiler_params=_cparams(("arbitrary", "arbitrary")),
        name="ssd_scan",
    )(xbc, xbc, xbc, packed, tr, xbc, xbc, xbc, packed, tr, e_mat)


def _ssd_out_kernel(yf_ref, yb_ref, xs_ref, z_ref, dsk_ref, gw_ref, w_ref, x_ref, gate_ref, o_ref):
    y = (yf_ref[...].astype(F32) + yb_ref[...].astype(F32)
         + xs_ref[...].astype(F32) * dsk_ref[...])
    z = z_ref[...].astype(F32)
    yn = _rms(y * (z * _sigmoid(z)), gw_ref[...]).astype(BF16)
    o_ref[...] = x_ref[...] + gate_ref[...] * _dot(yn, w_ref[...])


def _ssd_out_call(geom, layer, y_f, y_b, xbc, z, d_skip_e, gnorm_w, w_out, x_all, mod3):
    tm = min(geom.tm, 512)
    rows = geom.rows
    return pl.pallas_call(
        _ssd_out_kernel,
        grid=(rows // tm,),
        in_specs=[
            pl.BlockSpec((tm, D_INNER), lambda i: (i, 0)),
            pl.BlockSpec((tm, D_INNER), lambda i: (i, 0)),
            pl.BlockSpec((tm, D_INNER), lambda i: (i, 0)),
            pl.BlockSpec((tm, D_INNER), lambda i: (i, 0)),
            pl.BlockSpec((1, D_INNER), lambda i: (0, 0)),
            pl.BlockSpec((1, D_INNER), lambda i: (0, 0)),
            _resident((D_INNER, D_MODEL), lambda i: (0, 0)),
            pl.BlockSpec((tm, D_MODEL), lambda i: (i, 0)),
            geom.mod_spec(layer, 2, tm),
        ],
        out_specs=pl.BlockSpec((tm, D_MODEL), lambda i: (i, 0)),
        out_shape=jax.ShapeDtypeStruct((rows, D_MODEL), F32),
        compiler_params=_cparams(("arbitrary",)),
        name="ssd_out_proj",
    )(y_f, y_b, xbc, z, d_skip_e, gnorm_w, w_out, x_all, mod3)


def _rope_blocks(acc, cos, sin):
    outs = []
    for c in range(acc.shape[1] // LANES):
        xb = acc[:, c * LANES:(c + 1) * LANES]
        outs.append(xb * cos + pltpu.roll(xb, LANES // 2, 1) * sin)
    return outs


def _qkv_kernel(x_ref, nw_ref, sh_ref, sc_ref, w_ref, cos_ref, sin_ref, q1_ref, q2_ref, k_ref, v_ref):
    a = _rms(x_ref[...], nw_ref[...]) * (1.0 + sc_ref[...]) + sh_ref[...]
    ab = a.astype(BF16)
    cos, sin = cos_ref[...], sin_ref[...]
    scale = DA_HEAD_DIM ** -0.5 * LOG2E
    lane = lax.broadcasted_iota(jnp.int32, (ab.shape[0], LANES), 1)
    is_map1 = (lane % (LANES // 2)) < (LANES // 4)
    for c0 in range(0, D_MODEL, PROJ_TILE):
        acc = _dot(ab, w_ref[:, c0:c0 + PROJ_TILE])
        for c, blk in enumerate(_rope_blocks(acc, cos, sin)):
            blk = blk * scale
            col = c0 + c * LANES
            q1_ref[:, col:col + LANES] = jnp.where(is_map1, blk, 0.0).astype(BF16)
            q2_ref[:, col:col + LANES] = jnp.where(is_map1, 0.0, blk).astype(BF16)
    for c0 in range(0, D_MODEL, PROJ_TILE):
        acc = _dot(ab, w_ref[:, D_MODEL + c0:D_MODEL + c0 + PROJ_TILE])
        for c, blk in enumerate(_rope_blocks(acc, cos, sin)):
            col = c0 + c * LANES
            k_ref[:, col:col + LANES] = blk.astype(BF16)
    for c0 in range(0, D_MODEL, PROJ_TILE):
        v_ref[:, c0:c0 + PROJ_TILE] = _dot(
            ab, w_ref[:, 2 * D_MODEL + c0:2 * D_MODEL + c0 + PROJ_TILE]).astype(BF16)


def _qkv_call(geom, layer, x_all, mod3, norm_w, w_qkv, cos_t, sin_t):
    tm = min(geom.tm, 512)
    rows = geom.rows
    row_out = pl.BlockSpec((tm, D_MODEL), lambda i: (i, 0))
    return pl.pallas_call(
        _qkv_kernel,
        grid=(rows // tm,),
        in_specs=[
            pl.BlockSpec((tm, D_MODEL), lambda i: (i, 0)),
            pl.BlockSpec((1, D_MODEL), lambda i: (0, 0)),
            geom.mod_spec(layer, 0, tm),
            geom.mod_spec(layer, 1, tm),
            _resident((D_MODEL, 3 * D_MODEL), lambda i: (0, 0)),
            pl.BlockSpec((tm, LANES), lambda i: (i, 0)),
            pl.BlockSpec((tm, LANES), lambda i: (i, 0)),
        ],
        out_specs=[row_out, row_out, row_out, row_out],
        out_shape=[jax.ShapeDtypeStruct((rows, D_MODEL), BF16)] * 4,
        compiler_params=_cparams(("arbitrary",)),
        name="attn_qkv_rope",
    )(x_all, norm_w, mod3, mod3, w_qkv, cos_t, sin_t)


FLASH_ROWS = 64


def _flash_kernel(lam_ref, subw_ref, q1_ref, q2_ref, *refs, seg_lens, tk, lambda_init):
    n_seg = len(seg_lens)
    kv_refs = refs[:2 * n_seg]
    o_ref, s_scr, p_scr, acc_scr, m_scr, al_scr = refs[2 * n_seg + 1:]
    tq = q1_ref.shape[0]
    qs = (q1_ref[...], q2_ref[...])
    acc_scr[...] = jnp.zeros_like(acc_scr)
    m_scr[...] = jnp.full_like(m_scr, -jnp.inf)

    def scores(slot, k_blk):
        w = k_blk.shape[0]
        for t in range(2):
            s_scr[slot, t, :, :w] = _dot_nt(qs[t], k_blk)

    def absorb(slot, v_blk):
        w = v_blk.shape[0]
        vaug = jnp.concatenate([v_blk, jnp.ones_like(v_blk)], axis=1)
        for t in range(2):
            for r in range(0, tq, FLASH_ROWS):
                rows = pl.ds(r, FLASH_ROWS)
                tiles = [s_scr[slot, t, rows, c:c + LANES] for c in range(0, w, LANES)]
                mx = tiles[0]
                for tl in tiles[1:]:
                    mx = jnp.maximum(mx, tl)
                m_old = m_scr[t, rows, :]
                m_new = jnp.maximum(m_old, jnp.max(mx, axis=1, keepdims=True))
                for ci, tl in enumerate(tiles):
                    p_scr[t, rows, ci * LANES:(ci + 1) * LANES] = jnp.exp2(tl - m_new).astype(BF16)
                m_scr[t, rows, :] = m_new
                al_scr[t, rows, :] = jnp.exp2(m_old - m_new)
            alpha = al_scr[t]
            acc_scr[t] = (jnp.concatenate([alpha, alpha], axis=1) * acc_scr[t]
                          + _dot(p_scr[t, :, :w], vaug))

    if n_seg == 1:
        scores(0, kv_refs[0][...])
        absorb(0, kv_refs[1][...])
    else:
        kl_ref, vl_ref, kc_ref, vc_ref = kv_refs
        n_lat = seg_lens[0] // tk

        def k_block(i):
            return kc_ref[...] if i == n_lat else kl_ref[i * tk:(i + 1) * tk, :]

        def v_block(i):
            return vc_ref[...] if i == n_lat else vl_ref[i * tk:(i + 1) * tk, :]

        def step(i, slot):
            if i + 1 <= n_lat:
                scores(1 - slot, k_block(i + 1))
            absorb(slot, v_block(i))

        scores(0, k_block(0))
        for i in range(n_lat + 1):
            step(i, i % 2)

    lam_v = lam_ref[...]
    hd = DA_HEAD_DIM
    lam = (jnp.exp(jnp.sum(lam_v[:, 0:hd] * lam_v[:, hd:2 * hd], axis=1, keepdims=True))
           - jnp.exp(jnp.sum(lam_v[:, 2 * hd:3 * hd] * lam_v[:, 3 * hd:4 * hd], axis=1, keepdims=True))
           + lambda_init)
    a1 = acc_scr[0]
    a2 = acc_scr[1]
    o = a1[:, :LANES] / a1[:, LANES:] - lam * (a2[:, :LANES] / a2[:, LANES:])
    o = _rms(o, subw_ref[...]) * (1.0 - lambda_init)
    o_ref[...] = o.astype(BF16)


def _flash_call(geom, q1, q2, k, v, lam_vec, subln_w, lambda_init, o_prev, ctx_queries):
    bsz, seq, ctx = geom.bsz, geom.seq, geom.ctx
    tk = 512
    if ctx_queries:
        tq = min(256, ctx)
        nq = ctx // tq
        q_base = geom.n_lat // tq
        seg_lens = (ctx,)
    else:
        tq = min(256, seq)
        nq = seq // tq
        q_base = 0
        seg_lens = (seq, ctx)
    ctx_base = geom.n_lat // ctx

    def q_map(b, h, i):
        return (q_base + b * nq + i, h)

    kv_lat = pl.BlockSpec((seq, LANES), lambda b, h, i: (b, h))
    kv_ctx = pl.BlockSpec((ctx, LANES), lambda b, h, i: (ctx_base + b, h))
    kv_specs = [kv_ctx, kv_ctx] if ctx_queries else [kv_lat, kv_lat, kv_ctx, kv_ctx]
    kv_args = [k, v] if ctx_queries else [k, v, k, v]
    kern = functools.partial(_flash_kernel, seg_lens=seg_lens, tk=tk, lambda_init=lambda_init)
    n_in = 4 + len(kv_args)
    return pl.pallas_call(
        kern,
        grid=(bsz, DA_HEADS, nq),
        in_specs=[
            pl.BlockSpec((1, 4 * DA_HEAD_DIM), lambda b, h, i: (0, 0)),
            pl.BlockSpec((1, LANES), lambda b, h, i: (0, 0)),
            pl.BlockSpec((tq, LANES), q_map),
            pl.BlockSpec((tq, LANES), q_map),
            *kv_specs,
            pl.BlockSpec(memory_space=pl.ANY),
        ],
        out_specs=pl.BlockSpec((tq, LANES), q_map),
        out_shape=jax.ShapeDtypeStruct((geom.rows, D_MODEL), BF16),
        input_output_aliases={n_in: 0},
        scratch_shapes=[pltpu.VMEM((2, 2, tq, tk), F32),
                        pltpu.VMEM((2, tq, tk), BF16),
                        pltpu.VMEM((2, tq, 2 * LANES), F32),
                        pltpu.VMEM((2, tq, LANES), F32),
                        pltpu.VMEM((2, tq, LANES), F32)],
        compiler_params=_cparams(("arbitrary", "arbitrary", "arbitrary")),
        name="flash_ctx" if ctx_queries else "flash_lat",
    )(lam_vec, subln_w, q1, q2, *kv_args, o_prev)


def _attn_out_kernel(o_ref, w_ref, x_ref, gate_ref, out_ref):
    out_ref[...] = x_ref[...] + gate_ref[...] * _dot(o_ref[...], w_ref[...])


def _attn_out_call(geom, layer, o, w_o, x_all, mod3):
    tm = geom.tm
    rows = geom.rows
    return pl.pallas_call(
        _attn_out_kernel,
        grid=(rows // tm,),
        in_specs=[
            pl.BlockSpec((tm, D_MODEL), lambda i: (i, 0)),
            _resident((D_MODEL, D_MODEL), lambda i: (0, 0)),
            pl.BlockSpec((tm, D_MODEL), lambda i: (i, 0)),
            geom.mod_spec(layer, 2, tm),
        ],
        out_specs=pl.BlockSpec((tm, D_MODEL), lambda i: (i, 0)),
        out_shape=jax.ShapeDtypeStruct((rows, D_MODEL), F32),
        compiler_params=_cparams(("arbitrary",)),
        name="attn_out_proj",
    )(o, w_o, x_all, mod3)


FFN_TILE = 256


def _ffn_kernel(x_ref, nw_ref, sh_ref, sc_ref, gate_ref, wg_ref, wu_ref, wd_ref, fw_ref,
                o_ref, h_scr, *, final_norm):
    x = x_ref[...]
    a = (_rms(x, nw_ref[...]) * (1.0 + sc_ref[...]) + sh_ref[...]).astype(BF16)
    for f in range(0, D_FF, FFN_TILE):
        g = _dot(a, wg_ref[:, f:f + FFN_TILE])
        u = _dot(a, wu_ref[:, f:f + FFN_TILE])
        h_scr[:, f:f + FFN_TILE] = (g * _sigmoid(g) * u).astype(BF16)
    out = x + gate_ref[...] * _dot(h_scr[...], wd_ref[...])
    if final_norm:
        out = _rms(out, fw_ref[...])
    o_ref[...] = out


def _ffn_call(geom, layer, x_all, mod3, norm_w, w_gate, w_up, w_down, final_w, final_norm):
    tm = min(geom.tm, 512)
    rows = geom.n_lat if final_norm else geom.rows
    kern = functools.partial(_ffn_kernel, final_norm=final_norm)
    return pl.pallas_call(
        kern,
        grid=(rows // tm,),
        in_specs=[
            pl.BlockSpec((tm, D_MODEL), lambda i: (i, 0)),
            pl.BlockSpec((1, D_MODEL), lambda i: (0, 0)),
            geom.mod_spec(layer, 3, tm),
            geom.mod_spec(layer, 4, tm),
            geom.mod_spec(layer, 5, tm),
            _resident((D_MODEL, D_FF), lambda i: (0, 0)),
            _resident((D_MODEL, D_FF), lambda i: (0, 0)),
            _resident((D_FF, D_MODEL), lambda i: (0, 0)),
            pl.BlockSpec((1, D_MODEL), lambda i: (0, 0)),
        ],
        out_specs=pl.BlockSpec((tm, D_MODEL), lambda i: (i, 0)),
        out_shape=jax.ShapeDtypeStruct((rows, D_MODEL), F32),
        scratch_shapes=[pltpu.VMEM((tm, D_FF), BF16)],
        compiler_params=_cparams(("arbitrary",)),
        name="ffn_swiglu",
    )(x_all, norm_w, mod3, mod3, mod3, w_gate, w_up, w_down, final_w)


def _rope_tables(geom):
    seq = geom.seq
    t = jnp.arange(seq)
    row = (t // GRID_W).astype(F32)
    col = (t % GRID_W).astype(F32)
    freqs = 1.0 / (ROPE_THETA ** (jnp.arange(ROPE_PAIRS, dtype=F32) / ROPE_PAIRS))
    ang = jnp.concatenate([row[:, None] * freqs, col[:, None] * freqs], axis=-1)
    cos, sin = jnp.cos(ang), jnp.sin(ang)
    cos_l = jnp.tile(cos, (geom.bsz, 4))
    sin_l = jnp.tile(jnp.concatenate([-sin, -sin, sin, sin], axis=-1), (geom.bsz, 1))
    cos_t = jnp.concatenate([cos_l, jnp.ones((geom.n_ctx, LANES), F32)], axis=0)
    sin_t = jnp.concatenate([sin_l, jnp.zeros((geom.n_ctx, LANES), F32)], axis=0)
    return cos_t, sin_t


def _qk_col_perm():
    perm = []
    quarter = DA_HEAD_DIM // 2
    for h in range(DA_HEADS):
        for half in range(2):
            for m in range(2):
                base = h * 2 * DA_HEAD_DIM + m * DA_HEAD_DIM + half * quarter
                perm.extend(range(base, base + quarter))
    return jnp.asarray(perm, dtype=jnp.int32)


def kernel(x, c, ctx, c_ctx, w_ada, b_ada, norm1_w, norm2_w, ssd_w_in, ssd_conv_w, ssd_conv_b,
           ssd_a_log, ssd_dt_bias, ssd_d, ssd_norm_w, ssd_w_out, da_w_qkv, da_w_o, da_lq1, da_lk1,
           da_lq2, da_lk2, da_subln_w, ffn_w_gate, ffn_w_up, ffn_w_down, final_norm_w):
    bsz, seq, d = x.shape
    n_ctx_tok = ctx.shape[1]
    assert d == D_MODEL and bsz + 1 <= MOD_ROWS
    assert seq % CONV_ROWS == 0 and n_ctx_tok % CONV_ROWS == 0 and seq % GRID_W == 0
    geom = _Geom(bsz, seq, n_ctx_tok)

    x_all = jnp.concatenate([x.reshape(bsz * seq, d), ctx.reshape(bsz * n_ctx_tok, d)], axis=0)
    cvec = jnp.concatenate([c, c_ctx[None, :], jnp.zeros((MOD_ROWS - bsz - 1, d), F32)], axis=0)
    mod = _ada_call(cvec, w_ada.astype(BF16), b_ada)
    mod3 = mod.reshape(DEPTH * MOD_ROWS * 6, 1, D_MODEL)

    cos_t, sin_t = _rope_tables(geom)
    perm = _qk_col_perm()
    hq = DA_HEADS * 2 * DA_HEAD_DIM
    e_mat = (jnp.arange(D_INNER)[None, :] // SSD_HEAD_DIM == jnp.arange(SSD_HEADS)[:, None]).astype(BF16)
    e_mat = jnp.concatenate([e_mat, e_mat], axis=0)

    out = None
    for i in range(DEPTH):
        j = i // 2
        last = i == DEPTH - 1
        n1 = norm1_w[i].reshape(1, d)
        if i % 2 == 0:
            w_in = ssd_w_in[j]
            w_zx = w_in[:, :D_ZX].astype(BF16)
            w_dt = w_in[:, D_ZX:].reshape(d, 2, SSD_HEADS).transpose(1, 0, 2)
            w_dt = jnp.pad(w_dt, ((0, 0), (0, 0), (0, LANES - SSD_HEADS))).astype(BF16)
            dt_b = jnp.pad(ssd_dt_bias[j], ((0, 0), (0, LANES - SSD_HEADS))).reshape(2, 1, LANES)
            z, xbc, packed, tr = _ssd_in_call(geom, i, x_all, mod3, n1, w_zx, w_dt, dt_b,
                                              ssd_a_log[j].reshape(2, 1, SSD_HEADS), ssd_conv_w[j],
                                              ssd_conv_b[j].reshape(1, D_CONV_CH))
            y_f, y_b = _ssd_scan_call(geom, xbc, packed, tr, e_mat)
            d_skip_e = jnp.repeat(ssd_d[j], SSD_HEAD_DIM).reshape(1, D_INNER)
            x_all = _ssd_out_call(geom, i, y_f, y_b, xbc, z, d_skip_e, ssd_norm_w[j].reshape(1, D_INNER),
                                  ssd_w_out[j].astype(BF16), x_all, mod3)
        else:
            lambda_init = 0.8 - 0.6 * math.exp(-0.3 * i)
            w_qkv = da_w_qkv[j]
            w_qkv = jnp.concatenate([w_qkv[:, :hq][:, perm], w_qkv[:, hq:2 * hq][:, perm],
                                     w_qkv[:, 2 * hq:]], axis=1).astype(BF16)
            q1, q2, k, v = _qkv_call(geom, i, x_all, mod3, n1, w_qkv, cos_t, sin_t)
            lam_vec = jnp.concatenate([da_lq1[j], da_lk1[j], da_lq2[j], da_lk2[j]]).reshape(1, 4 * DA_HEAD_DIM)
            subw = da_subln_w[j].reshape(1, LANES)
            o = jnp.zeros((geom.rows, D_MODEL), BF16)
            o = _flash_call(geom, q1, q2, k, v, lam_vec, subw, lambda_init, o, ctx_queries=False)
            if not last:
                o = _flash_call(geom, q1, q2, k, v, lam_vec, subw, lambda_init, o, ctx_queries=True)
            x_all = _attn_out_call(geom, i, o, da_w_o[j].astype(BF16), x_all, mod3)
        res = _ffn_call(geom, i, x_all, mod3, norm2_w[i].reshape(1, d), ffn_w_gate[i].astype(BF16),
                        ffn_w_up[i].astype(BF16), ffn_w_down[i].astype(BF16),
                        final_norm_w.reshape(1, d), final_norm=last)
        if last:
            out = res
        else:
            x_all = res
    return out.reshape(bsz, seq, d)
```

```python
import functools
import math

import jax
import jax.numpy as jnp
from jax import lax
from jax.experimental import pallas as pl
from jax.experimental.pallas import tpu as pltpu

F32 = jnp.float32
BF16 = jnp.bfloat16

D_MODEL = 1024
DEPTH = 4
GRID_W = 64
EPS = 1e-6

D_INNER = 2 * D_MODEL
SSD_HEAD_DIM = 64
SSD_HEADS = D_INNER // SSD_HEAD_DIM
SSD_GROUPS = 4
HEADS_PER_GROUP = SSD_HEADS // SSD_GROUPS
D_STATE = 128
CONV_W = 5
CHUNK = 128
D_BC = SSD_GROUPS * D_STATE
D_CONV_CH = D_INNER + 2 * D_BC
D_ZX = D_INNER + D_CONV_CH

DA_HEADS = D_MODEL // 128
DA_HEAD_DIM = 64
ROPE_THETA = 10000.0
ROPE_PAIRS = DA_HEAD_DIM // 4

D_FF = -(-8 * D_MODEL // (3 * 256)) * 256

LANES = 128
LOG2E = math.log2(math.e)
MOD_ROWS = 16
VMEM_LIMIT = 56 * 1024 * 1024


def _cparams(sem):
    return pltpu.CompilerParams(dimension_semantics=sem, vmem_limit_bytes=VMEM_LIMIT)


def _dot(a, b):
    return jnp.dot(a, b, preferred_element_type=F32)


def _dot_nt(a, b):
    return lax.dot_general(a, b, (((1,), (1,)), ((), ())), preferred_element_type=F32)


def _sigmoid(x):
    return 1.0 / (1.0 + jnp.exp(-x))


def _rms(xf, w):
    return xf * lax.rsqrt(jnp.mean(xf * xf, axis=-1, keepdims=True) + EPS) * w


def _resident(shape, index_map):
    return pl.BlockSpec(shape, index_map, pipeline_mode=pl.Buffered(1))


def _ada_kernel(c_ref, w_ref, b_ref, o_ref):
    cv = c_ref[...]
    s = (cv * _sigmoid(cv)).astype(BF16)
    o_ref[...] = _dot(s, w_ref[...].astype(BF16)) + b_ref[...]


def _ada_call(cvec, w_ada, b_ada):
    depth, d, n = w_ada.shape
    tn = 2048
    return pl.pallas_call(
        _ada_kernel,
        grid=(depth, n // tn),
        in_specs=[
            pl.BlockSpec((MOD_ROWS, d), lambda l, j: (0, 0)),
            pl.BlockSpec((None, d, tn), lambda l, j: (l, 0, j)),
            pl.BlockSpec((None, 1, tn), lambda l, j: (l, 0, j)),
        ],
        out_specs=pl.BlockSpec((None, MOD_ROWS, tn), lambda l, j: (l, 0, j)),
        out_shape=jax.ShapeDtypeStruct((depth, MOD_ROWS, n), F32),
        compiler_params=_cparams(("arbitrary", "arbitrary")),
        name="adaln",
    )(cvec, w_ada, b_ada.reshape(depth, 1, n))


class _Geom:
    def __init__(self, bsz, seq, ctx):
        self.bsz, self.seq, self.ctx = bsz, seq, ctx
        self.n_lat = bsz * seq
        self.n_ctx = bsz * ctx
        self.rows = self.n_lat + self.n_ctx
        tm = 1024
        while seq % tm or self.n_ctx % tm:
            tm //= 2
        self.tm = tm

    def mod_spec(self, layer, k, tm):
        n_lat_tiles = self.n_lat // tm
        per_batch = self.seq // tm
        bsz = self.bsz

        def imap(i, *_):
            row = jnp.where(i < n_lat_tiles, i // per_batch, bsz)
            return ((layer * MOD_ROWS + row) * 6 + k, 0, 0)

        return pl.BlockSpec((None, 1, D_MODEL), imap)


def _softplus(x):
    return jnp.maximum(x, 0.0) + jnp.log1p(jnp.exp(-jnp.abs(x)))


PROJ_TILE = 512


def _split3(v):
    p1 = v.astype(BF16)
    r1 = v - p1.astype(F32)
    p2 = r1.astype(BF16)
    p3 = (r1 - p2.astype(F32)).astype(BF16)
    return p1, p2, p3


def _scan_mask(backward):
    li = lax.broadcasted_iota(jnp.int32, (CHUNK, CHUNK), 0)
    si = lax.broadcasted_iota(jnp.int32, (CHUNK, CHUNK), 1)
    return si >= li if backward else si <= li


def _scan_terms(backward, dt, alog):
    nh = SSD_HEADS
    a = -jnp.exp(alog)
    adt = dt * (a * LOG2E)
    mask = _scan_mask(backward)
    cum3 = _dot(mask.astype(F32).astype(BF16), jnp.concatenate(_split3(adt), axis=1))
    cum = cum3[:, :nh] + cum3[:, nh:2 * nh] + cum3[:, 2 * nh:3 * nh]
    tot = jnp.sum(adt, axis=0, keepdims=True)
    pad = jnp.zeros((CHUNK, CHUNK - nh), F32)
    tr = jnp.concatenate([cum - jnp.log2(dt), pad], axis=1).T[:nh]
    packed = jnp.concatenate([cum, dt * jnp.exp2(tot - cum), jnp.exp2(cum),
                              jnp.broadcast_to(jnp.exp2(tot), (CHUNK, nh))], axis=1)
    return packed, tr


CONV_ROWS = 256
HALO = 16


def _ssd_in_kernel(xp_ref, x_ref, xn_ref, nw_ref, sh_ref, sc_ref, w_ref, wdt_ref, dtb_ref, alog_ref,
                   cw_ref, cb_ref, z_ref, xbc_ref, pk_ref, tr_ref, *, n_lat, seq, ctx):
    def norm_mod(v):
        return (_rms(v, nw_ref[...]) * (1.0 + sc_ref[...]) + sh_ref[...]).astype(BF16)

    ab = norm_mod(x_ref[...])
    tm = ab.shape[0]
    a_ext = jnp.concatenate([norm_mod(xp_ref[...]), ab, norm_mod(xn_ref[...])], axis=0)
    for d in range(2):
        dt = _softplus(_dot(ab, wdt_ref[d]) + dtb_ref[d])
        for c in range(tm // CHUNK):
            packed, tr = _scan_terms(d == 1, dt[c * CHUNK:(c + 1) * CHUNK, :SSD_HEADS], alog_ref[d])
            pk_ref[d, c * CHUNK:(c + 1) * CHUNK, :] = packed
            tr_ref[d, c * SSD_HEADS:(c + 1) * SSD_HEADS, :] = tr
    for c in range(0, D_INNER, PROJ_TILE):
        z_ref[:, c:c + PROJ_TILE] = _dot(ab, w_ref[:, c:c + PROJ_TILE]).astype(BF16)

    row0 = pl.program_id(0) * tm
    keep = []
    for s in range(tm // CONV_ROWS):
        g0 = row0 + s * CONV_ROWS
        g1 = g0 + CONV_ROWS
        starts = jnp.where(g0 < n_lat, g0 % seq == 0, (g0 - n_lat) % ctx == 0)
        ends = jnp.where(g1 <= n_lat, g1 % seq == 0, (g1 - n_lat) % ctx == 0)
        keep.append((jnp.where(starts, 0.0, 1.0), jnp.where(ends, 0.0, 1.0)))
    n_ext = CONV_ROWS + 16
    mid = CONV_W // 2
    for c in range(0, D_CONV_CH, PROJ_TILE):
        acc = _dot(a_ext, w_ref[:, D_INNER + c:D_INNER + c + PROJ_TILE])
        cw = cw_ref[:, c:c + PROJ_TILE]
        cbias = cb_ref[:, c:c + PROJ_TILE]
        for s in range(tm // CONV_ROWS):
            base = HALO + s * CONV_ROWS
            ext = jnp.concatenate([acc[base - 8:base] * keep[s][0], acc[base:base + CONV_ROWS],
                                   acc[base + CONV_ROWS:base + CONV_ROWS + 8] * keep[s][1]], axis=0)
            y = cbias + cw[mid:mid + 1, :] * ext[8:8 + CONV_ROWS]
            for k in range(CONV_W):
                if k != mid:
                    rolled = pltpu.roll(ext, (mid - k) % n_ext, 0)
                    y = y + cw[k:k + 1, :] * rolled[8:8 + CONV_ROWS]
            xbc_ref[s * CONV_ROWS:(s + 1) * CONV_ROWS, c:c + PROJ_TILE] = (y * _sigmoid(y)).astype(BF16)


def _ssd_in_call(geom, layer, x_all, mod3, norm_w, w_zx, w_dt, dt_bias, a_log, conv_w, conv_b):
    tm = min(geom.tm, 512)
    rows = geom.rows
    tr_rows = tm // CHUNK * SSD_HEADS
    hb = tm // HALO
    n_halo = rows // HALO
    kern = functools.partial(_ssd_in_kernel, n_lat=geom.n_lat, seq=geom.seq, ctx=geom.ctx)
    return pl.pallas_call(
        kern,
        grid=(rows // tm,),
        in_specs=[
            pl.BlockSpec((HALO, D_MODEL), lambda i: (jnp.maximum(i * hb - 1, 0), 0)),
            pl.BlockSpec((tm, D_MODEL), lambda i: (i, 0)),
            pl.BlockSpec((HALO, D_MODEL), lambda i: (jnp.minimum((i + 1) * hb, n_halo - 1), 0)),
            pl.BlockSpec((1, D_MODEL), lambda i: (0, 0)),
            geom.mod_spec(layer, 0, tm),
            geom.mod_spec(layer, 1, tm),
            _resident((D_MODEL, D_ZX), lambda i: (0, 0)),
            pl.BlockSpec((2, D_MODEL, LANES), lambda i: (0, 0, 0)),
            pl.BlockSpec((2, 1, LANES), lambda i: (0, 0, 0)),
            pl.BlockSpec((2, 1, SSD_HEADS), lambda i: (0, 0, 0)),
            pl.BlockSpec((CONV_W, D_CONV_CH), lambda i: (0, 0)),
            pl.BlockSpec((1, D_CONV_CH), lambda i: (0, 0)),
        ],
        out_specs=[
            pl.BlockSpec((tm, D_INNER), lambda i: (i, 0)),
            pl.BlockSpec((tm, D_CONV_CH), lambda i: (i, 0)),
            pl.BlockSpec((2, tm, LANES), lambda i: (0, i, 0)),
            pl.BlockSpec((2, tr_rows, LANES), lambda i: (0, i, 0)),
        ],
        out_shape=[
            jax.ShapeDtypeStruct((rows, D_INNER), BF16),
            jax.ShapeDtypeStruct((rows, D_CONV_CH), BF16),
            jax.ShapeDtypeStruct((2, rows, LANES), F32),
            jax.ShapeDtypeStruct((2, rows // CHUNK * SSD_HEADS, LANES), F32),
        ],
        compiler_params=_cparams(("arbitrary",)),
        name="ssd_in_proj_conv",
    )(x_all, x_all, x_all, norm_w, mod3, mod3, w_zx, w_dt, dt_bias, a_log, conv_w, conv_b)


def _expand_heads(v, e2_ref):
    hi = v.astype(BF16)
    lo = (v - hi.astype(F32)).astype(BF16)
    return _dot(jnp.concatenate([hi, lo], axis=1), e2_ref[...])


def _scan_prep(backward, pk_ref, tr_ref, e_ref):
    nh = SSD_HEADS
    pk = pk_ref[...]
    cum = pk[:, :nh]
    ex = _expand_heads(jnp.concatenate([pk[:, nh:2 * nh], pk[:, 2 * nh:3 * nh], pk[:8, 3 * nh:]], axis=0),
                       e_ref)
    return (_scan_mask(backward), cum, tr_ref[...], ex[:CHUNK], ex[CHUNK:2 * CHUNK],
            ex[2 * CHUNK:2 * CHUNK + 1])


def _scan_group(g, prep, xs_ref, b_ref, c_ref, y_ref, h_ref):
    mask, cum, tr, wst_e, dout_e, cdec_e = prep
    lane = lax.broadcasted_iota(jnp.int32, (CHUNK, 2 * SSD_HEAD_DIM), 1)
    gw = HEADS_PER_GROUP * SSD_HEAD_DIM
    bg = b_ref[:, g * D_STATE:(g + 1) * D_STATE]
    cg = c_ref[:, g * D_STATE:(g + 1) * D_STATE]
    cb = _dot_nt(cg, bg)
    h_old = h_ref[:, g * gw:(g + 1) * gw]
    y_off = _dot(cg, h_old.astype(BF16)) * dout_e[:, g * gw:(g + 1) * gw]
    xg = xs_ref[:, g * gw:(g + 1) * gw]
    for q in range(HEADS_PER_GROUP // 2):
        ws = []
        for hh in range(2):
            h = g * HEADS_PER_GROUP + 2 * q + hh
            diff = cum[:, h:h + 1] - tr[h:h + 1, :]
            lmat = jnp.exp2(jnp.where(mask, diff, -jnp.inf))
            ws.append((cb * lmat).astype(BF16))
        xp = xg[:, q * 2 * SSD_HEAD_DIM:(q + 1) * 2 * SSD_HEAD_DIM]
        zero = jnp.zeros_like(xp)
        rhs = jnp.concatenate([jnp.where(lane < SSD_HEAD_DIM, xp, zero),
                               jnp.where(lane >= SSD_HEAD_DIM, xp, zero)], axis=0)
        yd = _dot(jnp.concatenate(ws, axis=1), rhs)
        c0 = g * gw + q * 2 * SSD_HEAD_DIM
        y_ref[:, c0:c0 + 2 * SSD_HEAD_DIM] = (
            yd + y_off[:, q * 2 * SSD_HEAD_DIM:(q + 1) * 2 * SSD_HEAD_DIM]).astype(y_ref.dtype)
    xw = (xg.astype(F32) * wst_e[:, g * gw:(g + 1) * gw]).astype(BF16)
    bgt = bg.astype(F32).T.astype(BF16)
    h_ref[:, g * gw:(g + 1) * gw] = h_old * cdec_e[:, g * gw:(g + 1) * gw] + _dot(bgt, xw)


def _ssd_scan_kernel(xf_ref, bf_ref, cf_ref, pkf_ref, trf_ref, xb_ref, bb_ref, cb_ref, pkb_ref, trb_ref,
                     e_ref, yf_ref, yb_ref, h_ref):
    @pl.when(pl.program_id(1) == 0)
    def _():
        h_ref[...] = jnp.zeros_like(h_ref)

    prep_f = _scan_prep(False, pkf_ref, trf_ref, e_ref)
    prep_b = _scan_prep(True, pkb_ref, trb_ref, e_ref)
    for g in range(SSD_GROUPS):
        _scan_group(g, prep_f, xf_ref, bf_ref, cf_ref, yf_ref, h_ref.at[0])
        _scan_group(g, prep_b, xb_ref, bb_ref, cb_ref, yb_ref, h_ref.at[1])


def _ssd_scan_call(geom, xbc, packed, tr, e_mat):
    rows = geom.rows
    nc_lat = geom.seq // CHUNK
    nc_ctx = geom.ctx // CHUNK
    lat_blocks = geom.n_lat // CHUNK

    def chunk_block(d, b, s):
        in_ctx = s < nc_ctx
        sl = s - nc_ctx
        c_ctx = nc_ctx - 1 - s if d else s
        c_lat = nc_lat - 1 - sl if d else sl
        return jnp.where(in_ctx, lat_blocks + b * nc_ctx + c_ctx, b * nc_lat + c_lat)

    xcols = D_INNER // D_BC

    def dir_specs(d):
        return [
            pl.BlockSpec((CHUNK, D_INNER), lambda b, s: (chunk_block(d, b, s), 0)),
            pl.BlockSpec((CHUNK, D_BC), lambda b, s: (chunk_block(d, b, s), xcols)),
            pl.BlockSpec((CHUNK, D_BC), lambda b, s: (chunk_block(d, b, s), xcols + 1)),
            pl.BlockSpec((None, CHUNK, LANES), lambda b, s: (d, chunk_block(d, b, s), 0)),
            pl.BlockSpec((None, SSD_HEADS, LANES), lambda b, s: (d, chunk_block(d, b, s), 0)),
        ]

    y_shape = jax.ShapeDtypeStruct((rows, D_INNER), BF16)
    return pl.pallas_call(
        _ssd_scan_kernel,
        grid=(geom.bsz, nc_ctx + nc_lat),
        in_specs=dir_specs(0) + dir_specs(1) + [
            pl.BlockSpec((2 * SSD_HEADS, D_INNER), lambda b, s: (0, 0)),
        ],
        out_specs=[pl.BlockSpec((CHUNK, D_INNER), lambda b, s: (chunk_block(0, b, s), 0)),
                   pl.BlockSpec((CHUNK, D_INNER), lambda b, s: (chunk_block(1, b, s), 0))],
        out_shape=[y_shape, y_shape],
        scratch_shapes=[pltpu.VMEM((2, D_STATE, D_INNER), F32)],
        compiler_params=_cparams(("arbitrary", "arbitrary")),
        name="ssd_scan",
    )(xbc, xbc, xbc, packed, tr, xbc, xbc, xbc, packed, tr, e_mat)


def _ssd_out_kernel(yf_ref, yb_ref, xs_ref, z_ref, dsk_ref, gw_ref, w_ref, x_ref, gate_ref, o_ref):
    y = (yf_ref[...].astype(F32) + yb_ref[...].astype(F32)
         + xs_ref[...].astype(F32) * dsk_ref[...])
    z = z_ref[...].astype(F32)
    yn = _rms(y * (z * _sigmoid(z)), gw_ref[...]).astype(BF16)
    o_ref[...] = x_ref[...] + gate_ref[...] * _dot(yn, w_ref[...])


def _ssd_out_call(geom, layer, y_f, y_b, xbc, z, d_skip_e, gnorm_w, w_out, x_all, mod3):
    tm = min(geom.tm, 512)
    rows = geom.rows
    return pl.pallas_call(
        _ssd_out_kernel,
        grid=(rows // tm,),
        in_specs=[
            pl.BlockSpec((tm, D_INNER), lambda i: (i, 0)),
            pl.BlockSpec((tm, D_INNER), lambda i: (i, 0)),
            pl.BlockSpec((tm, D_INNER), lambda i: (i, 0)),
            pl.BlockSpec((tm, D_INNER), lambda i: (i, 0)),
            pl.BlockSpec((1, D_INNER), lambda i: (0, 0)),
            pl.BlockSpec((1, D_INNER), lambda i: (0, 0)),
            _resident((D_INNER, D_MODEL), lambda i: (0, 0)),
            pl.BlockSpec((tm, D_MODEL), lambda i: (i, 0)),
            geom.mod_spec(layer, 2, tm),
        ],
        out_specs=pl.BlockSpec((tm, D_MODEL), lambda i: (i, 0)),
        out_shape=jax.ShapeDtypeStruct((rows, D_MODEL), F32),
        compiler_params=_cparams(("arbitrary",)),
        name="ssd_out_proj",
    )(y_f, y_b, xbc, z, d_skip_e, gnorm_w, w_out, x_all, mod3)


def _rope_blocks(acc, cos, sin):
    outs = []
    for c in range(acc.shape[1] // LANES):
        xb = acc[:, c * LANES:(c + 1) * LANES]
        outs.append(xb * cos + pltpu.roll(xb, LANES // 2, 1) * sin)
    return outs


def _qkv_kernel(x_ref, nw_ref, sh_ref, sc_ref, w_ref, cos_ref, sin_ref, q1_ref, q2_ref, k_ref, v_ref):
    a = _rms(x_ref[...], nw_ref[...]) * (1.0 + sc_ref[...]) + sh_ref[...]
    ab = a.astype(BF16)
    cos, sin = cos_ref[...], sin_ref[...]
    scale = DA_HEAD_DIM ** -0.5 * LOG2E
    lane = lax.broadcasted_iota(jnp.int32, (ab.shape[0], LANES), 1)
    is_map1 = (lane % (LANES // 2)) < (LANES // 4)
    for c0 in range(0, D_MODEL, PROJ_TILE):
        acc = _dot(ab, w_ref[:, c0:c0 + PROJ_TILE])
        for c, blk in enumerate(_rope_blocks(acc, cos, sin)):
            blk = blk * scale
            col = c0 + c * LANES
            q1_ref[:, col:col + LANES] = jnp.where(is_map1, blk, 0.0).astype(BF16)
            q2_ref[:, col:col + LANES] = jnp.where(is_map1, 0.0, blk).astype(BF16)
    for c0 in range(0, D_MODEL, PROJ_TILE):
        acc = _dot(ab, w_ref[:, D_MODEL + c0:D_MODEL + c0 + PROJ_TILE])
        for c, blk in enumerate(_rope_blocks(acc, cos, sin)):
            col = c0 + c * LANES
            k_ref[:, col:col + LANES] = blk.astype(BF16)
    for c0 in range(0, D_MODEL, PROJ_TILE):
        v_ref[:, c0:c0 + PROJ_TILE] = _dot(
            ab, w_ref[:, 2 * D_MODEL + c0:2 * D_MODEL + c0 + PROJ_TILE]).astype(BF16)


def _qkv_call(geom, layer, x_all, mod3, norm_w, w_qkv, cos_t, sin_t):
    tm = min(geom.tm, 512)
    rows = geom.rows
    row_out = pl.BlockSpec((tm, D_MODEL), lambda i: (i, 0))
    return pl.pallas_call(
        _qkv_kernel,
        grid=(rows // tm,),
        in_specs=[
            pl.BlockSpec((tm, D_MODEL), lambda i: (i, 0)),
            pl.BlockSpec((1, D_MODEL), lambda i: (0, 0)),
            geom.mod_spec(layer, 0, tm),
            geom.mod_spec(layer, 1, tm),
            _resident((D_MODEL, 3 * D_MODEL), lambda i: (0, 0)),
            pl.BlockSpec((tm, LANES), lambda i: (i, 0)),
            pl.BlockSpec((tm, LANES), lambda i: (i, 0)),
        ],
        out_specs=[row_out, row_out, row_out, row_out],
        out_shape=[jax.ShapeDtypeStruct((rows, D_MODEL), BF16)] * 4,
        compiler_params=_cparams(("arbitrary",)),
        name="attn_qkv_rope",
    )(x_all, norm_w, mod3, mod3, w_qkv, cos_t, sin_t)


FLASH_ROWS = 64


def _flash_kernel(lam_ref, subw_ref, q1_ref, q2_ref, *refs, seg_lens, tk, tq, aliased, lambda_init):
    n_seg = len(seg_lens)
    kv_refs = refs[:2 * n_seg]
    o_ref, s_scr, p_scr, acc_scr, m_scr, al_scr = refs[2 * n_seg + (1 if aliased else 0):]

    lam_v = lam_ref[...]
    hd = DA_HEAD_DIM
    lam = (jnp.exp(jnp.sum(lam_v[:, 0:hd] * lam_v[:, hd:2 * hd], axis=1, keepdims=True))
           - jnp.exp(jnp.sum(lam_v[:, 2 * hd:3 * hd] * lam_v[:, 3 * hd:4 * hd], axis=1, keepdims=True))
           + lambda_init)

    def q_tile(qrows):
        qs = (q1_ref[qrows, :], q2_ref[qrows, :])
        acc_scr[...] = jnp.zeros_like(acc_scr)
        m_scr[...] = jnp.full_like(m_scr, -jnp.inf)

        def scores(slot, k_blk):
            w = k_blk.shape[0]
            for t in range(2):
                s_scr[slot, t, :, :w] = _dot_nt(qs[t], k_blk)

        def absorb(slot, v_blk):
            w = v_blk.shape[0]
            vaug = jnp.concatenate([v_blk, jnp.ones_like(v_blk)], axis=1)
            for t in range(2):
                for r in range(0, tq, FLASH_ROWS):
                    rows = pl.ds(r, FLASH_ROWS)
                    tiles = [s_scr[slot, t, rows, c:c + LANES] for c in range(0, w, LANES)]
                    mx = tiles[0]
                    for tl in tiles[1:]:
                        mx = jnp.maximum(mx, tl)
                    m_old = m_scr[t, rows, :]
                    m_new = jnp.maximum(m_old, jnp.max(mx, axis=1, keepdims=True))
                    for ci, tl in enumerate(tiles):
                        p_scr[t, rows, ci * LANES:(ci + 1) * LANES] = jnp.exp2(tl - m_new).astype(BF16)
                    m_scr[t, rows, :] = m_new
                    al_scr[t, rows, :] = jnp.exp2(m_old - m_new)
                alpha = al_scr[t]
                acc_scr[t] = (jnp.concatenate([alpha, alpha], axis=1) * acc_scr[t]
                              + _dot(p_scr[t, :, :w], vaug))

        if n_seg == 1:
            scores(0, kv_refs[0][...])
            absorb(0, kv_refs[1][...])
        else:
            kl_ref, vl_ref, kc_ref, vc_ref = kv_refs
            n_lat = seg_lens[0] // tk

            def k_block(i):
                return kc_ref[...] if i == n_lat else kl_ref[i * tk:(i + 1) * tk, :]

            def v_block(i):
                return vc_ref[...] if i == n_lat else vl_ref[i * tk:(i + 1) * tk, :]

            scores(0, k_block(0))
            for i in range(n_lat + 1):
                if i + 1 <= n_lat:
                    scores((i + 1) % 2, k_block(i + 1))
                absorb(i % 2, v_block(i))

        a1 = acc_scr[0]
        a2 = acc_scr[1]
        o = a1[:, :LANES] / a1[:, LANES:] - lam * (a2[:, :LANES] / a2[:, LANES:])
        o = _rms(o, subw_ref[...]) * (1.0 - lambda_init)
        o_ref[qrows, :] = o.astype(BF16)

    n_sub = q1_ref.shape[0] // tq
    if n_sub == 1:
        q_tile(pl.ds(0, tq))
    else:
        def body(qi, carry):
            q_tile(pl.ds(pl.multiple_of(qi * tq, tq), tq))
            return carry
        lax.fori_loop(0, n_sub, body, 0)


Q_SUBTILES = 4


def _flash_call(geom, q1, q2, k, v, lam_vec, subln_w, lambda_init, o_prev, ctx_queries, out_rows):
    bsz, seq, ctx = geom.bsz, geom.seq, geom.ctx
    tk = 512
    if ctx_queries:
        tq = min(256, ctx)
        qb = tq
        nq = ctx // qb
        q_base = geom.n_lat // qb
        seg_lens = (ctx,)
    else:
        tq = min(256, seq)
        qb = min(Q_SUBTILES * tq, seq)
        nq = seq // qb
        q_base = 0
        seg_lens = (seq, ctx)
    ctx_base = geom.n_lat // ctx
    aliased = o_prev is not None

    def q_map(b, h, i):
        return (q_base + b * nq + i, h)

    kv_lat = pl.BlockSpec((seq, LANES), lambda b, h, i: (b, h))
    kv_ctx = pl.BlockSpec((ctx, LANES), lambda b, h, i: (ctx_base + b, h))
    kv_specs = [kv_ctx, kv_ctx] if ctx_queries else [kv_lat, kv_lat, kv_ctx, kv_ctx]
    kv_args = [k, v] if ctx_queries else [k, v, k, v]
    kern = functools.partial(_flash_kernel, seg_lens=seg_lens, tk=tk, tq=tq, aliased=aliased,
                             lambda_init=lambda_init)
    n_in = 4 + len(kv_args)
    return pl.pallas_call(
        kern,
        grid=(bsz, DA_HEADS, nq),
        in_specs=[
            pl.BlockSpec((1, 4 * DA_HEAD_DIM), lambda b, h, i: (0, 0)),
            pl.BlockSpec((1, LANES), lambda b, h, i: (0, 0)),
            pl.BlockSpec((qb, LANES), q_map),
            pl.BlockSpec((qb, LANES), q_map),
            *kv_specs,
        ] + ([pl.BlockSpec(memory_space=pl.ANY)] if aliased else []),
        out_specs=pl.BlockSpec((qb, LANES), q_map),
        out_shape=jax.ShapeDtypeStruct((out_rows, D_MODEL), BF16),
        input_output_aliases={n_in: 0} if aliased else {},
        scratch_shapes=[pltpu.VMEM((2, 2, tq, tk), F32),
                        pltpu.VMEM((2, tq, tk), BF16),
                        pltpu.VMEM((2, tq, 2 * LANES), F32),
                        pltpu.VMEM((2, tq, LANES), F32),
                        pltpu.VMEM((2, tq, LANES), F32)],
        compiler_params=_cparams(("arbitrary", "arbitrary", "arbitrary")),
        name="flash_ctx" if ctx_queries else "flash_lat",
    )(lam_vec, subln_w, q1, q2, *kv_args, *([o_prev] if aliased else []))


def _attn_out_kernel(o_ref, w_ref, x_ref, gate_ref, out_ref):
    out_ref[...] = x_ref[...] + gate_ref[...] * _dot(o_ref[...], w_ref[...])


def _attn_out_call(geom, layer, o, w_o, x_all, mod3, rows):
    tm = geom.tm
    return pl.pallas_call(
        _attn_out_kernel,
        grid=(rows // tm,),
        in_specs=[
            pl.BlockSpec((tm, D_MODEL), lambda i: (i, 0)),
            _resident((D_MODEL, D_MODEL), lambda i: (0, 0)),
            pl.BlockSpec((tm, D_MODEL), lambda i: (i, 0)),
            geom.mod_spec(layer, 2, tm),
        ],
        out_specs=pl.BlockSpec((tm, D_MODEL), lambda i: (i, 0)),
        out_shape=jax.ShapeDtypeStruct((rows, D_MODEL), F32),
        compiler_params=_cparams(("arbitrary",)),
        name="attn_out_proj",
    )(o, w_o, x_all, mod3)


FFN_TILE = 256


def _ffn_kernel(x_ref, nw_ref, sh_ref, sc_ref, gate_ref, wg_ref, wu_ref, wd_ref, fw_ref,
                o_ref, h_scr, *, final_norm):
    x = x_ref[...]
    a = (_rms(x, nw_ref[...]) * (1.0 + sc_ref[...]) + sh_ref[...]).astype(BF16)
    for f in range(0, D_FF, FFN_TILE):
        g = _dot(a, wg_ref[:, f:f + FFN_TILE])
        u = _dot(a, wu_ref[:, f:f + FFN_TILE])
        h_scr[:, f:f + FFN_TILE] = (g * _sigmoid(g) * u).astype(BF16)
    out = x + gate_ref[...] * _dot(h_scr[...], wd_ref[...])
    if final_norm:
        out = _rms(out, fw_ref[...])
    o_ref[...] = out


def _ffn_call(geom, layer, x_all, mod3, norm_w, w_gate, w_up, w_down, final_w, final_norm):
    tm = min(geom.tm, 512)
    rows = geom.n_lat if final_norm else geom.rows
    kern = functools.partial(_ffn_kernel, final_norm=final_norm)
    return pl.pallas_call(
        kern,
        grid=(rows // tm,),
        in_specs=[
            pl.BlockSpec((tm, D_MODEL), lambda i: (i, 0)),
            pl.BlockSpec((1, D_MODEL), lambda i: (0, 0)),
            geom.mod_spec(layer, 3, tm),
            geom.mod_spec(layer, 4, tm),
            geom.mod_spec(layer, 5, tm),
            _resident((D_MODEL, D_FF), lambda i: (0, 0)),
            _resident((D_MODEL, D_FF), lambda i: (0, 0)),
            _resident((D_FF, D_MODEL), lambda i: (0, 0)),
            pl.BlockSpec((1, D_MODEL), lambda i: (0, 0)),
        ],
        out_specs=pl.BlockSpec((tm, D_MODEL), lambda i: (i, 0)),
        out_shape=jax.ShapeDtypeStruct((rows, D_MODEL), F32),
        scratch_shapes=[pltpu.VMEM((tm, D_FF), BF16)],
        compiler_params=_cparams(("arbitrary",)),
        name="ffn_swiglu",
    )(x_all, norm_w, mod3, mod3, mod3, w_gate, w_up, w_down, final_w)


def _rope_tables(geom):
    seq = geom.seq
    t = jnp.arange(seq)
    row = (t // GRID_W).astype(F32)
    col = (t % GRID_W).astype(F32)
    freqs = 1.0 / (ROPE_THETA ** (jnp.arange(ROPE_PAIRS, dtype=F32) / ROPE_PAIRS))
    ang = jnp.concatenate([row[:, None] * freqs, col[:, None] * freqs], axis=-1)
    cos, sin = jnp.cos(ang), jnp.sin(ang)
    cos_l = jnp.tile(cos, (geom.bsz, 4))
    sin_l = jnp.tile(jnp.concatenate([-sin, -sin, sin, sin], axis=-1), (geom.bsz, 1))
    cos_t = jnp.concatenate([cos_l, jnp.ones((geom.n_ctx, LANES), F32)], axis=0)
    sin_t = jnp.concatenate([sin_l, jnp.zeros((geom.n_ctx, LANES), F32)], axis=0)
    return cos_t, sin_t


def _qk_col_perm():
    perm = []
    quarter = DA_HEAD_DIM // 2
    for h in range(DA_HEADS):
        for half in range(2):
            for m in range(2):
                base = h * 2 * DA_HEAD_DIM + m * DA_HEAD_DIM + half * quarter
                perm.extend(range(base, base + quarter))
    return jnp.asarray(perm, dtype=jnp.int32)


def kernel(x, c, ctx, c_ctx, w_ada, b_ada, norm1_w, norm2_w, ssd_w_in, ssd_conv_w, ssd_conv_b,
           ssd_a_log, ssd_dt_bias, ssd_d, ssd_norm_w, ssd_w_out, da_w_qkv, da_w_o, da_lq1, da_lk1,
           da_lq2, da_lk2, da_subln_w, ffn_w_gate, ffn_w_up, ffn_w_down, final_norm_w):
    bsz, seq, d = x.shape
    n_ctx_tok = ctx.shape[1]
    assert d == D_MODEL and bsz + 1 <= MOD_ROWS
    assert seq % CONV_ROWS == 0 and n_ctx_tok % CONV_ROWS == 0 and seq % GRID_W == 0
    geom = _Geom(bsz, seq, n_ctx_tok)

    x_all = jnp.concatenate([x.reshape(bsz * seq, d), ctx.reshape(bsz * n_ctx_tok, d)], axis=0)
    cvec = jnp.concatenate([c, c_ctx[None, :], jnp.zeros((MOD_ROWS - bsz - 1, d), F32)], axis=0)
    mod = _ada_call(cvec, w_ada, b_ada)
    mod3 = mod.reshape(DEPTH * MOD_ROWS * 6, 1, D_MODEL)

    cos_t, sin_t = _rope_tables(geom)
    perm = _qk_col_perm()
    hq = DA_HEADS * 2 * DA_HEAD_DIM
    e_mat = (jnp.arange(D_INNER)[None, :] // SSD_HEAD_DIM == jnp.arange(SSD_HEADS)[:, None]).astype(BF16)
    e_mat = jnp.concatenate([e_mat, e_mat], axis=0)

    out = None
    for i in range(DEPTH):
        j = i // 2
        last = i == DEPTH - 1
        n1 = norm1_w[i].reshape(1, d)
        if i % 2 == 0:
            w_in = ssd_w_in[j]
            w_zx = w_in[:, :D_ZX].astype(BF16)
            w_dt = w_in[:, D_ZX:].reshape(d, 2, SSD_HEADS).transpose(1, 0, 2)
            w_dt = jnp.pad(w_dt, ((0, 0), (0, 0), (0, LANES - SSD_HEADS))).astype(BF16)
            dt_b = jnp.pad(ssd_dt_bias[j], ((0, 0), (0, LANES - SSD_HEADS))).reshape(2, 1, LANES)
            z, xbc, packed, tr = _ssd_in_call(geom, i, x_all, mod3, n1, w_zx, w_dt, dt_b,
                                              ssd_a_log[j].reshape(2, 1, SSD_HEADS), ssd_conv_w[j],
                                              ssd_conv_b[j].reshape(1, D_CONV_CH))
            y_f, y_b = _ssd_scan_call(geom, xbc, packed, tr, e_mat)
            d_skip_e = jnp.repeat(ssd_d[j], SSD_HEAD_DIM).reshape(1, D_INNER)
            x_all = _ssd_out_call(geom, i, y_f, y_b, xbc, z, d_skip_e, ssd_norm_w[j].reshape(1, D_INNER),
                                  ssd_w_out[j].astype(BF16), x_all, mod3)
        else:
            lambda_init = 0.8 - 0.6 * math.exp(-0.3 * i)
            w_qkv = da_w_qkv[j]
            w_qkv = jnp.concatenate([w_qkv[:, :hq][:, perm], w_qkv[:, hq:2 * hq][:, perm],
                                     w_qkv[:, 2 * hq:]], axis=1).astype(BF16)
            q1, q2, k, v = _qkv_call(geom, i, x_all, mod3, n1, w_qkv, cos_t, sin_t)
            lam_vec = jnp.concatenate([da_lq1[j], da_lk1[j], da_lq2[j], da_lk2[j]]).reshape(1, 4 * DA_HEAD_DIM)
            subw = da_subln_w[j].reshape(1, LANES)
            o_rows = geom.n_lat if last else geom.rows
            o = _flash_call(geom, q1, q2, k, v, lam_vec, subw, lambda_init, None, False, o_rows)
            if not last:
                o = _flash_call(geom, q1, q2, k, v, lam_vec, subw, lambda_init, o, True, o_rows)
            x_all = _attn_out_call(geom, i, o, da_w_o[j].astype(BF16), x_all, mod3, o_rows)
        res = _ffn_call(geom, i, x_all, mod3, norm2_w[i].reshape(1, d), ffn_w_gate[i].astype(BF16),
                        ffn_w_up[i].astype(BF16), ffn_w_down[i].astype(BF16),
                        final_norm_w.reshape(1, d), final_norm=last)
        if last:
            out = res
        else:
            x_all = res
    return out.reshape(bsz, seq, d)
```

```python
import functools
import math

import jax
import jax.numpy as jnp
from jax import lax
from jax.experimental import pallas as pl
from jax.experimental.pallas import tpu as pltpu

F32 = jnp.float32
BF16 = jnp.bfloat16

D_MODEL = 1024
DEPTH = 4
GRID_W = 64
EPS = 1e-6

D_INNER = 2 * D_MODEL
SSD_HEAD_DIM = 64
SSD_HEADS = D_INNER // SSD_HEAD_DIM
SSD_GROUPS = 4
HEADS_PER_GROUP = SSD_HEADS // SSD_GROUPS
D_STATE = 128
CONV_W = 5
CHUNK = 128
D_BC = SSD_GROUPS * D_STATE
D_CONV_CH = D_INNER + 2 * D_BC
D_ZX = D_INNER + D_CONV_CH

DA_HEADS = D_MODEL // 128
DA_HEAD_DIM = 64
ROPE_THETA = 10000.0
ROPE_PAIRS = DA_HEAD_DIM // 4

D_FF = -(-8 * D_MODEL // (3 * 256)) * 256

LANES = 128
LOG2E = math.log2(math.e)
MOD_ROWS = 16
VMEM_LIMIT = 56 * 1024 * 1024


def _cparams(sem):
    return pltpu.CompilerParams(dimension_semantics=sem, vmem_limit_bytes=VMEM_LIMIT)


def _dot(a, b):
    return jnp.dot(a, b, preferred_element_type=F32)


def _dot_nt(a, b):
    return lax.dot_general(a, b, (((1,), (1,)), ((), ())), preferred_element_type=F32)


def _sigmoid(x):
    return 1.0 / (1.0 + jnp.exp(-x))


def _rms(xf, w):
    return xf * lax.rsqrt(jnp.mean(xf * xf, axis=-1, keepdims=True) + EPS) * w


def _resident(shape, index_map):
    return pl.BlockSpec(shape, index_map, pipeline_mode=pl.Buffered(1))


def _ada_kernel(c_ref, w_ref, b_ref, o_ref):
    cv = c_ref[...]
    s = (cv * _sigmoid(cv)).astype(BF16)
    o_ref[...] = _dot(s, w_ref[...].astype(BF16)) + b_ref[...]


def _ada_call(cvec, w_ada, b_ada):
    depth, d, n = w_ada.shape
    tn = 2048
    return pl.pallas_call(
        _ada_kernel,
        grid=(depth, n // tn),
        in_specs=[
            pl.BlockSpec((MOD_ROWS, d), lambda l, j: (0, 0)),
            pl.BlockSpec((None, d, tn), lambda l, j: (l, 0, j)),
            pl.BlockSpec((None, 1, tn), lambda l, j: (l, 0, j)),
        ],
        out_specs=pl.BlockSpec((None, MOD_ROWS, tn), lambda l, j: (l, 0, j)),
        out_shape=jax.ShapeDtypeStruct((depth, MOD_ROWS, n), F32),
        compiler_params=_cparams(("arbitrary", "arbitrary")),
        name="adaln",
    )(cvec, w_ada, b_ada.reshape(depth, 1, n))


class _Geom:
    def __init__(self, bsz, seq, ctx):
        self.bsz, self.seq, self.ctx = bsz, seq, ctx
        self.n_lat = bsz * seq
        self.n_ctx = bsz * ctx
        self.rows = self.n_lat + self.n_ctx
        tm = 1024
        while seq % tm or self.n_ctx % tm:
            tm //= 2
        self.tm = tm

    def mod_spec(self, layer, k, tm):
        n_lat_tiles = self.n_lat // tm
        per_batch = self.seq // tm
        bsz = self.bsz

        def imap(i, *_):
            row = jnp.where(i < n_lat_tiles, i // per_batch, bsz)
            return ((layer * MOD_ROWS + row) * 6 + k, 0, 0)

        return pl.BlockSpec((None, 1, D_MODEL), imap)


def _softplus(x):
    return jnp.maximum(x, 0.0) + jnp.log1p(jnp.exp(-jnp.abs(x)))


PROJ_TILE = 512


def _split3(v):
    p1 = v.astype(BF16)
    r1 = v - p1.astype(F32)
    p2 = r1.astype(BF16)
    p3 = (r1 - p2.astype(F32)).astype(BF16)
    return p1, p2, p3


def _scan_mask(backward):
    li = lax.broadcasted_iota(jnp.int32, (CHUNK, CHUNK), 0)
    si = lax.broadcasted_iota(jnp.int32, (CHUNK, CHUNK), 1)
    return si >= li if backward else si <= li


def _scan_terms(backward, dt, alog):
    nh = SSD_HEADS
    a = -jnp.exp(alog)
    adt = dt * (a * LOG2E)
    mask = _scan_mask(backward)
    cum3 = _dot(mask.astype(F32).astype(BF16), jnp.concatenate(_split3(adt), axis=1))
    cum = cum3[:, :nh] + cum3[:, nh:2 * nh] + cum3[:, 2 * nh:3 * nh]
    tot = jnp.sum(adt, axis=0, keepdims=True)
    pad = jnp.zeros((CHUNK, CHUNK - nh), F32)
    tr = jnp.concatenate([cum - jnp.log2(dt), pad], axis=1).T[:nh]
    packed = jnp.concatenate([cum, dt * jnp.exp2(tot - cum), jnp.exp2(cum),
                              jnp.broadcast_to(jnp.exp2(tot), (CHUNK, nh))], axis=1)
    return packed, tr


CONV_ROWS = 256
HALO = 16


def _ssd_in_kernel(xp_ref, x_ref, xn_ref, nw_ref, sh_ref, sc_ref, w_ref, wdt_ref, dtb_ref, alog_ref,
                   cw_ref, cb_ref, z_ref, xbc_ref, pk_ref, tr_ref, *, n_lat, seq, ctx):
    def norm_mod(v):
        return (_rms(v, nw_ref[...]) * (1.0 + sc_ref[...]) + sh_ref[...]).astype(BF16)

    ab = norm_mod(x_ref[...])
    tm = ab.shape[0]
    a_ext = jnp.concatenate([norm_mod(xp_ref[...]), ab, norm_mod(xn_ref[...])], axis=0)
    for d in range(2):
        dt = _softplus(_dot(ab, wdt_ref[d]) + dtb_ref[d])
        for c in range(tm // CHUNK):
            packed, tr = _scan_terms(d == 1, dt[c * CHUNK:(c + 1) * CHUNK, :SSD_HEADS], alog_ref[d])
            pk_ref[d, c * CHUNK:(c + 1) * CHUNK, :] = packed
            tr_ref[d, c * SSD_HEADS:(c + 1) * SSD_HEADS, :] = tr
    for c in range(0, D_INNER, PROJ_TILE):
        z_ref[:, c:c + PROJ_TILE] = _dot(ab, w_ref[:, c:c + PROJ_TILE]).astype(BF16)

    row0 = pl.program_id(0) * tm
    keep = []
    for s in range(tm // CONV_ROWS):
        g0 = row0 + s * CONV_ROWS
        g1 = g0 + CONV_ROWS
        starts = jnp.where(g0 < n_lat, g0 % seq == 0, (g0 - n_lat) % ctx == 0)
        ends = jnp.where(g1 <= n_lat, g1 % seq == 0, (g1 - n_lat) % ctx == 0)
        keep.append((jnp.where(starts, 0.0, 1.0), jnp.where(ends, 0.0, 1.0)))
    n_ext = CONV_ROWS + 16
    mid = CONV_W // 2
    for c in range(0, D_CONV_CH, PROJ_TILE):
        acc = _dot(a_ext, w_ref[:, D_INNER + c:D_INNER + c + PROJ_TILE])
        cw = cw_ref[:, c:c + PROJ_TILE]
        cbias = cb_ref[:, c:c + PROJ_TILE]
        for s in range(tm // CONV_ROWS):
            base = HALO + s * CONV_ROWS
            ext = jnp.concatenate([acc[base - 8:base] * keep[s][0], acc[base:base + CONV_ROWS],
                                   acc[base + CONV_ROWS:base + CONV_ROWS + 8] * keep[s][1]], axis=0)
            y = cbias + cw[mid:mid + 1, :] * ext[8:8 + CONV_ROWS]
            for k in range(CONV_W):
                if k != mid:
                    rolled = pltpu.roll(ext, (mid - k) % n_ext, 0)
                    y = y + cw[k:k + 1, :] * rolled[8:8 + CONV_ROWS]
            xbc_ref[s * CONV_ROWS:(s + 1) * CONV_ROWS, c:c + PROJ_TILE] = (y * _sigmoid(y)).astype(BF16)


def _ssd_in_call(geom, layer, x_all, mod3, norm_w, w_zx, w_dt, dt_bias, a_log, conv_w, conv_b):
    tm = min(geom.tm, 512)
    rows = geom.rows
    tr_rows = tm // CHUNK * SSD_HEADS
    hb = tm // HALO
    n_halo = rows // HALO
    kern = functools.partial(_ssd_in_kernel, n_lat=geom.n_lat, seq=geom.seq, ctx=geom.ctx)
    return pl.pallas_call(
        kern,
        grid=(rows // tm,),
        in_specs=[
            pl.BlockSpec((HALO, D_MODEL), lambda i: (jnp.maximum(i * hb - 1, 0), 0)),
            pl.BlockSpec((tm, D_MODEL), lambda i: (i, 0)),
            pl.BlockSpec((HALO, D_MODEL), lambda i: (jnp.minimum((i + 1) * hb, n_halo - 1), 0)),
            pl.BlockSpec((1, D_MODEL), lambda i: (0, 0)),
            geom.mod_spec(layer, 0, tm),
            geom.mod_spec(layer, 1, tm),
            _resident((D_MODEL, D_ZX), lambda i: (0, 0)),
            pl.BlockSpec((2, D_MODEL, LANES), lambda i: (0, 0, 0)),
            pl.BlockSpec((2, 1, LANES), lambda i: (0, 0, 0)),
            pl.BlockSpec((2, 1, SSD_HEADS), lambda i: (0, 0, 0)),
            pl.BlockSpec((CONV_W, D_CONV_CH), lambda i: (0, 0)),
            pl.BlockSpec((1, D_CONV_CH), lambda i: (0, 0)),
        ],
        out_specs=[
            pl.BlockSpec((tm, D_INNER), lambda i: (i, 0)),
            pl.BlockSpec((tm, D_CONV_CH), lambda i: (i, 0)),
            pl.BlockSpec((2, tm, LANES), lambda i: (0, i, 0)),
            pl.BlockSpec((2, tr_rows, LANES), lambda i: (0, i, 0)),
        ],
        out_shape=[
            jax.ShapeDtypeStruct((rows, D_INNER), BF16),
            jax.ShapeDtypeStruct((rows, D_CONV_CH), BF16),
            jax.ShapeDtypeStruct((2, rows, LANES), F32),
            jax.ShapeDtypeStruct((2, rows // CHUNK * SSD_HEADS, LANES), F32),
        ],
        compiler_params=_cparams(("arbitrary",)),
        name="ssd_in_proj_conv",
    )(x_all, x_all, x_all, norm_w, mod3, mod3, w_zx, w_dt, dt_bias, a_log, conv_w, conv_b)


def _expand_heads(v, e2_ref):
    hi = v.astype(BF16)
    lo = (v - hi.astype(F32)).astype(BF16)
    return _dot(jnp.concatenate([hi, lo], axis=1), e2_ref[...])


def _scan_prep(backward, pk_ref, tr_ref, e_ref):
    nh = SSD_HEADS
    pk = pk_ref[...]
    cum = pk[:, :nh]
    ex = _expand_heads(jnp.concatenate([pk[:, nh:2 * nh], pk[:, 2 * nh:3 * nh], pk[:8, 3 * nh:]], axis=0),
                       e_ref)
    return (_scan_mask(backward), cum, tr_ref[...], ex[:CHUNK], ex[CHUNK:2 * CHUNK],
            ex[2 * CHUNK:2 * CHUNK + 1])


def _scan_group(g, prep, xs_ref, b_ref, c_ref, y_ref, h_ref):
    mask, cum, tr, wst_e, dout_e, cdec_e = prep
    lane = lax.broadcasted_iota(jnp.int32, (CHUNK, 2 * SSD_HEAD_DIM), 1)
    gw = HEADS_PER_GROUP * SSD_HEAD_DIM
    bg = b_ref[:, g * D_STATE:(g + 1) * D_STATE]
    cg = c_ref[:, g * D_STATE:(g + 1) * D_STATE]
    cb = _dot_nt(cg, bg)
    h_old = h_ref[:, g * gw:(g + 1) * gw]
    y_off = _dot(cg, h_old.astype(BF16)) * dout_e[:, g * gw:(g + 1) * gw]
    xg = xs_ref[:, g * gw:(g + 1) * gw]
    for q in range(HEADS_PER_GROUP // 2):
        ws = []
        for hh in range(2):
            h = g * HEADS_PER_GROUP + 2 * q + hh
            diff = cum[:, h:h + 1] - tr[h:h + 1, :]
            lmat = jnp.exp2(jnp.where(mask, diff, -jnp.inf))
            ws.append((cb * lmat).astype(BF16))
        xp = xg[:, q * 2 * SSD_HEAD_DIM:(q + 1) * 2 * SSD_HEAD_DIM]
        zero = jnp.zeros_like(xp)
        rhs = jnp.concatenate([jnp.where(lane < SSD_HEAD_DIM, xp, zero),
                               jnp.where(lane >= SSD_HEAD_DIM, xp, zero)], axis=0)
        yd = _dot(jnp.concatenate(ws, axis=1), rhs)
        c0 = g * gw + q * 2 * SSD_HEAD_DIM
        y_ref[:, c0:c0 + 2 * SSD_HEAD_DIM] = (
            yd + y_off[:, q * 2 * SSD_HEAD_DIM:(q + 1) * 2 * SSD_HEAD_DIM]).astype(y_ref.dtype)
    xw = (xg.astype(F32) * wst_e[:, g * gw:(g + 1) * gw]).astype(BF16)
    bgt = bg.astype(F32).T.astype(BF16)
    h_ref[:, g * gw:(g + 1) * gw] = h_old * cdec_e[:, g * gw:(g + 1) * gw] + _dot(bgt, xw)


def _ssd_scan_kernel(xf_ref, bf_ref, cf_ref, pkf_ref, trf_ref, xb_ref, bb_ref, cb_ref, pkb_ref, trb_ref,
                     e_ref, yf_ref, yb_ref, h_ref):
    @pl.when(pl.program_id(1) == 0)
    def _():
        h_ref[...] = jnp.zeros_like(h_ref)

    prep_f = _scan_prep(False, pkf_ref, trf_ref, e_ref)
    prep_b = _scan_prep(True, pkb_ref, trb_ref, e_ref)
    for g in range(SSD_GROUPS):
        _scan_group(g, prep_f, xf_ref, bf_ref, cf_ref, yf_ref, h_ref.at[0])
        _scan_group(g, prep_b, xb_ref, bb_ref, cb_ref, yb_ref, h_ref.at[1])


def _ssd_scan_call(geom, xbc, packed, tr, e_mat):
    rows = geom.rows
    nc_lat = geom.seq // CHUNK
    nc_ctx = geom.ctx // CHUNK
    lat_blocks = geom.n_lat // CHUNK

    def chunk_block(d, b, s):
        in_ctx = s < nc_ctx
        sl = s - nc_ctx
        c_ctx = nc_ctx - 1 - s if d else s
        c_lat = nc_lat - 1 - sl if d else sl
        return jnp.where(in_ctx, lat_blocks + b * nc_ctx + c_ctx, b * nc_lat + c_lat)

    xcols = D_INNER // D_BC

    def dir_specs(d):
        return [
            pl.BlockSpec((CHUNK, D_INNER), lambda b, s: (chunk_block(d, b, s), 0)),
            pl.BlockSpec((CHUNK, D_BC), lambda b, s: (chunk_block(d, b, s), xcols)),
            pl.BlockSpec((CHUNK, D_BC), lambda b, s: (chunk_block(d, b, s), xcols + 1)),
            pl.BlockSpec((None, CHUNK, LANES), lambda b, s: (d, chunk_block(d, b, s), 0)),
            pl.BlockSpec((None, SSD_HEADS, LANES), lambda b, s: (d, chunk_block(d, b, s), 0)),
        ]

    y_shape = jax.ShapeDtypeStruct((rows, D_INNER), BF16)
    return pl.pallas_call(
        _ssd_scan_kernel,
        grid=(geom.bsz, nc_ctx + nc_lat),
        in_specs=dir_specs(0) + dir_specs(1) + [
            pl.BlockSpec((2 * SSD_HEADS, D_INNER), lambda b, s: (0, 0)),
        ],
        out_specs=[pl.BlockSpec((CHUNK, D_INNER), lambda b, s: (chunk_block(0, b, s), 0)),
                   pl.BlockSpec((CHUNK, D_INNER), lambda b, s: (chunk_block(1, b, s), 0))],
        out_shape=[y_shape, y_shape],
        scratch_shapes=[pltpu.VMEM((2, D_STATE, D_INNER), F32)],
        compiler_params=_cparams(("arbitrary", "arbitrary")),
        name="ssd_scan",
    )(xbc, xbc, xbc, packed, tr, xbc, xbc, xbc, packed, tr, e_mat)


def _ssd_out_kernel(yf_ref, yb_ref, xs_ref, z_ref, dsk_ref, gw_ref, w_ref, x_ref, gate_ref, o_ref):
    y = (yf_ref[...].astype(F32) + yb_ref[...].astype(F32)
         + xs_ref[...].astype(F32) * dsk_ref[...])
    z = z_ref[...].astype(F32)
    yn = _rms(y * (z * _sigmoid(z)), gw_ref[...]).astype(BF16)
    o_ref[...] = x_ref[...] + gate_ref[...] * _dot(yn, w_ref[...])


def _ssd_out_call(geom, layer, y_f, y_b, xbc, z, d_skip_e, gnorm_w, w_out, x_all, mod3):
    tm = min(geom.tm, 512)
    rows = geom.rows
    return pl.pallas_call(
        _ssd_out_kernel,
        grid=(rows // tm,),
        in_specs=[
            pl.BlockSpec((tm, D_INNER), lambda i: (i, 0)),
            pl.BlockSpec((tm, D_INNER), lambda i: (i, 0)),
            pl.BlockSpec((tm, D_INNER), lambda i: (i, 0)),
            pl.BlockSpec((tm, D_INNER), lambda i: (i, 0)),
            pl.BlockSpec((1, D_INNER), lambda i: (0, 0)),
            pl.BlockSpec((1, D_INNER), lambda i: (0, 0)),
            _resident((D_INNER, D_MODEL), lambda i: (0, 0)),
            pl.BlockSpec((tm, D_MODEL), lambda i: (i, 0)),
            geom.mod_spec(layer, 2, tm),
        ],
        out_specs=pl.BlockSpec((tm, D_MODEL), lambda i: (i, 0)),
        out_shape=jax.ShapeDtypeStruct((rows, D_MODEL), F32),
        compiler_params=_cparams(("arbitrary",)),
        name="ssd_out_proj",
    )(y_f, y_b, xbc, z, d_skip_e, gnorm_w, w_out, x_all, mod3)


def _rope_blocks(acc, cos, sin):
    outs = []
    for c in range(acc.shape[1] // LANES):
        xb = acc[:, c * LANES:(c + 1) * LANES]
        outs.append(xb * cos + pltpu.roll(xb, LANES // 2, 1) * sin)
    return outs


def _qkv_kernel(x_ref, nw_ref, sh_ref, sc_ref, w_ref, cos_ref, sin_ref, q1_ref, q2_ref, k_ref, v_ref):
    a = _rms(x_ref[...], nw_ref[...]) * (1.0 + sc_ref[...]) + sh_ref[...]
    ab = a.astype(BF16)
    cos, sin = cos_ref[...], sin_ref[...]
    scale = DA_HEAD_DIM ** -0.5 * LOG2E
    lane = lax.broadcasted_iota(jnp.int32, (ab.shape[0], LANES), 1)
    is_map1 = (lane % (LANES // 2)) < (LANES // 4)
    for c0 in range(0, D_MODEL, PROJ_TILE):
        acc = _dot(ab, w_ref[:, c0:c0 + PROJ_TILE])
        for c, blk in enumerate(_rope_blocks(acc, cos, sin)):
            blk = blk * scale
            col = c0 + c * LANES
            q1_ref[:, col:col + LANES] = jnp.where(is_map1, blk, 0.0).astype(BF16)
            q2_ref[:, col:col + LANES] = jnp.where(is_map1, 0.0, blk).astype(BF16)
    for c0 in range(0, D_MODEL, PROJ_TILE):
        acc = _dot(ab, w_ref[:, D_MODEL + c0:D_MODEL + c0 + PROJ_TILE])
        for c, blk in enumerate(_rope_blocks(acc, cos, sin)):
            col = c0 + c * LANES
            k_ref[:, col:col + LANES] = blk.astype(BF16)
    for c0 in range(0, D_MODEL, PROJ_TILE):
        v_ref[:, c0:c0 + PROJ_TILE] = _dot(
            ab, w_ref[:, 2 * D_MODEL + c0:2 * D_MODEL + c0 + PROJ_TILE]).astype(BF16)


def _qkv_call(geom, layer, x_all, mod3, norm_w, w_qkv, cos_t, sin_t):
    tm = min(geom.tm, 512)
    rows = geom.rows
    row_out = pl.BlockSpec((tm, D_MODEL), lambda i: (i, 0))
    return pl.pallas_call(
        _qkv_kernel,
        grid=(rows // tm,),
        in_specs=[
            pl.BlockSpec((tm, D_MODEL), lambda i: (i, 0)),
            pl.BlockSpec((1, D_MODEL), lambda i: (0, 0)),
            geom.mod_spec(layer, 0, tm),
            geom.mod_spec(layer, 1, tm),
            _resident((D_MODEL, 3 * D_MODEL), lambda i: (0, 0)),
            pl.BlockSpec((tm, LANES), lambda i: (i, 0)),
            pl.BlockSpec((tm, LANES), lambda i: (i, 0)),
        ],
        out_specs=[row_out, row_out, row_out, row_out],
        out_shape=[jax.ShapeDtypeStruct((rows, D_MODEL), BF16)] * 4,
        compiler_params=_cparams(("arbitrary",)),
        name="attn_qkv_rope",
    )(x_all, norm_w, mod3, mod3, w_qkv, cos_t, sin_t)


FLASH_ROWS = 64


def _flash_kernel(lam_ref, subw_ref, q1_ref, q2_ref, *refs, seg_lens, tk, tq, aliased, lambda_init):
    n_seg = len(seg_lens)
    kv_refs = refs[:2 * n_seg]
    o_ref, s_scr, p_scr, acc_scr, m_scr, al_scr = refs[2 * n_seg + (1 if aliased else 0):]

    lam_v = lam_ref[...]
    hd = DA_HEAD_DIM
    lam = (jnp.exp(jnp.sum(lam_v[:, 0:hd] * lam_v[:, hd:2 * hd], axis=1, keepdims=True))
           - jnp.exp(jnp.sum(lam_v[:, 2 * hd:3 * hd] * lam_v[:, 3 * hd:4 * hd], axis=1, keepdims=True))
           + lambda_init)

    def q_tile(qrows):
        qs = (q1_ref[qrows, :], q2_ref[qrows, :])
        acc_scr[...] = jnp.zeros_like(acc_scr)
        m_scr[...] = jnp.full_like(m_scr, -jnp.inf)

        def scores(slot, k_blk):
            w = k_blk.shape[0]
            for t in range(2):
                s_scr[slot, t, :, :w] = _dot_nt(qs[t], k_blk)

        def absorb(slot, v_blk):
            w = v_blk.shape[0]
            vaug = jnp.concatenate([v_blk, jnp.ones_like(v_blk)], axis=1)
            for t in range(2):
                for r in range(0, tq, FLASH_ROWS):
                    rows = pl.ds(r, FLASH_ROWS)
                    tiles = [s_scr[slot, t, rows, c:c + LANES] for c in range(0, w, LANES)]
                    mx = tiles[0]
                    for tl in tiles[1:]:
                        mx = jnp.maximum(mx, tl)
                    m_old = m_scr[t, rows, :]
                    m_new = jnp.maximum(m_old, jnp.max(mx, axis=1, keepdims=True))
                    for ci, tl in enumerate(tiles):
                        p_scr[t, rows, ci * LANES:(ci + 1) * LANES] = jnp.exp2(tl - m_new).astype(BF16)
                    m_scr[t, rows, :] = m_new
                    al_scr[t, rows, :] = jnp.exp2(m_old - m_new)
                alpha = al_scr[t]
                acc_scr[t] = (jnp.concatenate([alpha, alpha], axis=1) * acc_scr[t]
                              + _dot(p_scr[t, :, :w], vaug))

        if n_seg == 1:
            scores(0, kv_refs[0][...])
            absorb(0, kv_refs[1][...])
        else:
            kl_ref, vl_ref, kc_ref, vc_ref = kv_refs
            n_lat = seg_lens[0] // tk

            def k_block(i):
                return kc_ref[...] if i == n_lat else kl_ref[i * tk:(i + 1) * tk, :]

            def v_block(i):
                return vc_ref[...] if i == n_lat else vl_ref[i * tk:(i + 1) * tk, :]

            scores(0, k_block(0))
            for i in range(n_lat + 1):
                if i + 1 <= n_lat:
                    scores((i + 1) % 2, k_block(i + 1))
                absorb(i % 2, v_block(i))

        a1 = acc_scr[0]
        a2 = acc_scr[1]
        o = a1[:, :LANES] / a1[:, LANES:] - lam * (a2[:, :LANES] / a2[:, LANES:])
        o = _rms(o, subw_ref[...]) * (1.0 - lambda_init)
        o_ref[qrows, :] = o.astype(BF16)

    n_sub = q1_ref.shape[0] // tq
    if n_sub == 1:
        q_tile(pl.ds(0, tq))
    else:
        def body(qi, carry):
            q_tile(pl.ds(pl.multiple_of(qi * tq, tq), tq))
            return carry
        lax.fori_loop(0, n_sub, body, 0)


Q_SUBTILES = 4
FLASH_KEYS = 4096


def _flash_call(geom, q1, q2, k, v, lam_vec, subln_w, lambda_init, o_prev, ctx_queries, out_rows):
    bsz, seq, ctx = geom.bsz, geom.seq, geom.ctx
    tk = ctx if ctx_queries else min(FLASH_KEYS, seq)
    assert seq % tk == 0 or ctx_queries
    if ctx_queries:
        tq = min(256, ctx)
        qb = tq
        nq = ctx // qb
        q_base = geom.n_lat // qb
        seg_lens = (ctx,)
    else:
        tq = min(256, seq)
        qb = min(Q_SUBTILES * tq, seq)
        nq = seq // qb
        q_base = 0
        seg_lens = (seq, ctx)
    ctx_base = geom.n_lat // ctx
    aliased = o_prev is not None

    def q_map(b, h, i):
        return (q_base + b * nq + i, h)

    kv_lat = pl.BlockSpec((seq, LANES), lambda b, h, i: (b, h))
    kv_ctx = pl.BlockSpec((ctx, LANES), lambda b, h, i: (ctx_base + b, h))
    kv_specs = [kv_ctx, kv_ctx] if ctx_queries else [kv_lat, kv_lat, kv_ctx, kv_ctx]
    kv_args = [k, v] if ctx_queries else [k, v, k, v]
    kern = functools.partial(_flash_kernel, seg_lens=seg_lens, tk=tk, tq=tq, aliased=aliased,
                             lambda_init=lambda_init)
    n_in = 4 + len(kv_args)
    return pl.pallas_call(
        kern,
        grid=(bsz, DA_HEADS, nq),
        in_specs=[
            pl.BlockSpec((1, 4 * DA_HEAD_DIM), lambda b, h, i: (0, 0)),
            pl.BlockSpec((1, LANES), lambda b, h, i: (0, 0)),
            pl.BlockSpec((qb, LANES), q_map),
            pl.BlockSpec((qb, LANES), q_map),
            *kv_specs,
        ] + ([pl.BlockSpec(memory_space=pl.ANY)] if aliased else []),
        out_specs=pl.BlockSpec((qb, LANES), q_map),
        out_shape=jax.ShapeDtypeStruct((out_rows, D_MODEL), BF16),
        input_output_aliases={n_in: 0} if aliased else {},
        scratch_shapes=[pltpu.VMEM((2, 2, tq, tk), F32),
                        pltpu.VMEM((2, tq, tk), BF16),
                        pltpu.VMEM((2, tq, 2 * LANES), F32),
                        pltpu.VMEM((2, tq, LANES), F32),
                        pltpu.VMEM((2, tq, LANES), F32)],
        compiler_params=_cparams(("arbitrary", "arbitrary", "arbitrary")),
        name="flash_ctx" if ctx_queries else "flash_lat",
    )(lam_vec, subln_w, q1, q2, *kv_args, *([o_prev] if aliased else []))


def _attn_out_kernel(o_ref, w_ref, x_ref, gate_ref, out_ref):
    out_ref[...] = x_ref[...] + gate_ref[...] * _dot(o_ref[...], w_ref[...])


def _attn_out_call(geom, layer, o, w_o, x_all, mod3, rows):
    tm = geom.tm
    return pl.pallas_call(
        _attn_out_kernel,
        grid=(rows // tm,),
        in_specs=[
            pl.BlockSpec((tm, D_MODEL), lambda i: (i, 0)),
            _resident((D_MODEL, D_MODEL), lambda i: (0, 0)),
            pl.BlockSpec((tm, D_MODEL), lambda i: (i, 0)),
            geom.mod_spec(layer, 2, tm),
        ],
        out_specs=pl.BlockSpec((tm, D_MODEL), lambda i: (i, 0)),
        out_shape=jax.ShapeDtypeStruct((rows, D_MODEL), F32),
        compiler_params=_cparams(("arbitrary",)),
        name="attn_out_proj",
    )(o, w_o, x_all, mod3)


FFN_TILE = 256


def _ffn_kernel(x_ref, nw_ref, sh_ref, sc_ref, gate_ref, wg_ref, wu_ref, wd_ref, fw_ref,
                o_ref, h_scr, *, final_norm):
    x = x_ref[...]
    a = (_rms(x, nw_ref[...]) * (1.0 + sc_ref[...]) + sh_ref[...]).astype(BF16)
    for f in range(0, D_FF, FFN_TILE):
        g = _dot(a, wg_ref[:, f:f + FFN_TILE])
        u = _dot(a, wu_ref[:, f:f + FFN_TILE])
        h_scr[:, f:f + FFN_TILE] = (g * _sigmoid(g) * u).astype(BF16)
    out = x + gate_ref[...] * _dot(h_scr[...], wd_ref[...])
    if final_norm:
        out = _rms(out, fw_ref[...])
    o_ref[...] = out


def _ffn_call(geom, layer, x_all, mod3, norm_w, w_gate, w_up, w_down, final_w, final_norm):
    tm = min(geom.tm, 512)
    rows = geom.n_lat if final_norm else geom.rows
    kern = functools.partial(_ffn_kernel, final_norm=final_norm)
    return pl.pallas_call(
        kern,
        grid=(rows // tm,),
        in_specs=[
            pl.BlockSpec((tm, D_MODEL), lambda i: (i, 0)),
            pl.BlockSpec((1, D_MODEL), lambda i: (0, 0)),
            geom.mod_spec(layer, 3, tm),
            geom.mod_spec(layer, 4, tm),
            geom.mod_spec(layer, 5, tm),
            _resident((D_MODEL, D_FF), lambda i: (0, 0)),
            _resident((D_MODEL, D_FF), lambda i: (0, 0)),
            _resident((D_FF, D_MODEL), lambda i: (0, 0)),
            pl.BlockSpec((1, D_MODEL), lambda i: (0, 0)),
        ],
        out_specs=pl.BlockSpec((tm, D_MODEL), lambda i: (i, 0)),
        out_shape=jax.ShapeDtypeStruct((rows, D_MODEL), F32),
        scratch_shapes=[pltpu.VMEM((tm, D_FF), BF16)],
        compiler_params=_cparams(("arbitrary",)),
        name="ffn_swiglu",
    )(x_all, norm_w, mod3, mod3, mod3, w_gate, w_up, w_down, final_w)


def _rope_tables(geom):
    seq = geom.seq
    t = jnp.arange(seq)
    row = (t // GRID_W).astype(F32)
    col = (t % GRID_W).astype(F32)
    freqs = 1.0 / (ROPE_THETA ** (jnp.arange(ROPE_PAIRS, dtype=F32) / ROPE_PAIRS))
    ang = jnp.concatenate([row[:, None] * freqs, col[:, None] * freqs], axis=-1)
    cos, sin = jnp.cos(ang), jnp.sin(ang)
    cos_l = jnp.tile(cos, (geom.bsz, 4))
    sin_l = jnp.tile(jnp.concatenate([-sin, -sin, sin, sin], axis=-1), (geom.bsz, 1))
    cos_t = jnp.concatenate([cos_l, jnp.ones((geom.n_ctx, LANES), F32)], axis=0)
    sin_t = jnp.concatenate([sin_l, jnp.zeros((geom.n_ctx, LANES), F32)], axis=0)
    return cos_t, sin_t


def _qk_col_perm():
    perm = []
    quarter = DA_HEAD_DIM // 2
    for h in range(DA_HEADS):
        for half in range(2):
            for m in range(2):
                base = h * 2 * DA_HEAD_DIM + m * DA_HEAD_DIM + half * quarter
                perm.extend(range(base, base + quarter))
    return jnp.asarray(perm, dtype=jnp.int32)


def kernel(x, c, ctx, c_ctx, w_ada, b_ada, norm1_w, norm2_w, ssd_w_in, ssd_conv_w, ssd_conv_b,
           ssd_a_log, ssd_dt_bias, ssd_d, ssd_norm_w, ssd_w_out, da_w_qkv, da_w_o, da_lq1, da_lk1,
           da_lq2, da_lk2, da_subln_w, ffn_w_gate, ffn_w_up, ffn_w_down, final_norm_w):
    bsz, seq, d = x.shape
    n_ctx_tok = ctx.shape[1]
    assert d == D_MODEL and bsz + 1 <= MOD_ROWS
    assert seq % CONV_ROWS == 0 and n_ctx_tok % CONV_ROWS == 0 and seq % GRID_W == 0
    geom = _Geom(bsz, seq, n_ctx_tok)

    x_all = jnp.concatenate([x.reshape(bsz * seq, d), ctx.reshape(bsz * n_ctx_tok, d)], axis=0)
    cvec = jnp.concatenate([c, c_ctx[None, :], jnp.zeros((MOD_ROWS - bsz - 1, d), F32)], axis=0)
    mod = _ada_call(cvec, w_ada, b_ada)
    mod3 = mod.reshape(DEPTH * MOD_ROWS * 6, 1, D_MODEL)

    cos_t, sin_t = _rope_tables(geom)
    perm = _qk_col_perm()
    hq = DA_HEADS * 2 * DA_HEAD_DIM
    e_mat = (jnp.arange(D_INNER)[None, :] // SSD_HEAD_DIM == jnp.arange(SSD_HEADS)[:, None]).astype(BF16)
    e_mat = jnp.concatenate([e_mat, e_mat], axis=0)

    out = None
    for i in range(DEPTH):
        j = i // 2
        last = i == DEPTH - 1
        n1 = norm1_w[i].reshape(1, d)
        if i % 2 == 0:
            w_in = ssd_w_in[j]
            w_zx = w_in[:, :D_ZX].astype(BF16)
            w_dt = w_in[:, D_ZX:].reshape(d, 2, SSD_HEADS).transpose(1, 0, 2)
            w_dt = jnp.pad(w_dt, ((0, 0), (0, 0), (0, LANES - SSD_HEADS))).astype(BF16)
            dt_b = jnp.pad(ssd_dt_bias[j], ((0, 0), (0, LANES - SSD_HEADS))).reshape(2, 1, LANES)
            z, xbc, packed, tr = _ssd_in_call(geom, i, x_all, mod3, n1, w_zx, w_dt, dt_b,
                                              ssd_a_log[j].reshape(2, 1, SSD_HEADS), ssd_conv_w[j],
                                              ssd_conv_b[j].reshape(1, D_CONV_CH))
            y_f, y_b = _ssd_scan_call(geom, xbc, packed, tr, e_mat)
            d_skip_e = jnp.repeat(ssd_d[j], SSD_HEAD_DIM).reshape(1, D_INNER)
            x_all = _ssd_out_call(geom, i, y_f, y_b, xbc, z, d_skip_e, ssd_norm_w[j].reshape(1, D_INNER),
                                  ssd_w_out[j].astype(BF16), x_all, mod3)
        else:
            lambda_init = 0.8 - 0.6 * math.exp(-0.3 * i)
            w_qkv = da_w_qkv[j]
            w_qkv = jnp.concatenate([w_qkv[:, :hq][:, perm], w_qkv[:, hq:2 * hq][:, perm],
                                     w_qkv[:, 2 * hq:]], axis=1).astype(BF16)
            q1, q2, k, v = _qkv_call(geom, i, x_all, mod3, n1, w_qkv, cos_t, sin_t)
            lam_vec = jnp.concatenate([da_lq1[j], da_lk1[j], da_lq2[j], da_lk2[j]]).reshape(1, 4 * DA_HEAD_DIM)
            subw = da_subln_w[j].reshape(1, LANES)
            o_rows = geom.n_lat if last else geom.rows
            o = _flash_call(geom, q1, q2, k, v, lam_vec, subw, lambda_init, None, False, o_rows)
            if not last:
                o = _flash_call(geom, q1, q2, k, v, lam_vec, subw, lambda_init, o, True, o_rows)
            x_all = _attn_out_call(geom, i, o, da_w_o[j].astype(BF16), x_all, mod3, o_rows)
        res = _ffn_call(geom, i, x_all, mod3, norm2_w[i].reshape(1, d), ffn_w_gate[i].astype(BF16),
                        ffn_w_up[i].astype(BF16), ffn_w_down[i].astype(BF16),
                        final_norm_w.reshape(1, d), final_norm=last)
        if last:
            out = res
        else:
            x_all = res
    return out.reshape(bsz, seq, d)
```

```python
import functools
import math

import jax
import jax.numpy as jnp
from jax import lax
from jax.experimental import pallas as pl
from jax.experimental.pallas import tpu as pltpu

F32 = jnp.float32
BF16 = jnp.bfloat16

D_MODEL = 1024
DEPTH = 4
GRID_W = 64
EPS = 1e-6

D_INNER = 2 * D_MODEL
SSD_HEAD_DIM = 64
SSD_HEADS = D_INNER // SSD_HEAD_DIM
SSD_GROUPS = 4
HEADS_PER_GROUP = SSD_HEADS // SSD_GROUPS
D_STATE = 128
CONV_W = 5
CHUNK = 128
D_BC = SSD_GROUPS * D_STATE
D_CONV_CH = D_INNER + 2 * D_BC
D_ZX = D_INNER + D_CONV_CH

DA_HEADS = D_MODEL // 128
DA_HEAD_DIM = 64
ROPE_THETA = 10000.0
ROPE_PAIRS = DA_HEAD_DIM // 4

D_FF = -(-8 * D_MODEL // (3 * 256)) * 256

LANES = 128
LOG2E = math.log2(math.e)
MOD_ROWS = 16
VMEM_LIMIT = 56 * 1024 * 1024


def _cparams(sem):
    return pltpu.CompilerParams(dimension_semantics=sem, vmem_limit_bytes=VMEM_LIMIT)


def _dot(a, b):
    return jnp.dot(a, b, preferred_element_type=F32)


def _dot_nt(a, b):
    return lax.dot_general(a, b, (((1,), (1,)), ((), ())), preferred_element_type=F32)


def _sigmoid(x):
    return 1.0 / (1.0 + jnp.exp(-x))


def _rms(xf, w):
    return xf * lax.rsqrt(jnp.mean(xf * xf, axis=-1, keepdims=True) + EPS) * w


def _resident(shape, index_map):
    return pl.BlockSpec(shape, index_map, pipeline_mode=pl.Buffered(1))


def _ada_kernel(c_ref, w_ref, b_ref, o_ref):
    cv = c_ref[...]
    s = (cv * _sigmoid(cv)).astype(BF16)
    o_ref[...] = _dot(s, w_ref[...].astype(BF16)) + b_ref[...]


def _ada_call(cvec, w_ada, b_ada):
    depth, d, n = w_ada.shape
    tn = 2048
    return pl.pallas_call(
        _ada_kernel,
        grid=(depth, n // tn),
        in_specs=[
            pl.BlockSpec((MOD_ROWS, d), lambda l, j: (0, 0)),
            pl.BlockSpec((None, d, tn), lambda l, j: (l, 0, j)),
            pl.BlockSpec((None, 1, tn), lambda l, j: (l, 0, j)),
        ],
        out_specs=pl.BlockSpec((None, MOD_ROWS, tn), lambda l, j: (l, 0, j)),
        out_shape=jax.ShapeDtypeStruct((depth, MOD_ROWS, n), F32),
        compiler_params=_cparams(("arbitrary", "arbitrary")),
        name="adaln",
    )(cvec, w_ada, b_ada.reshape(depth, 1, n))


class _Geom:
    def __init__(self, bsz, seq, ctx):
        self.bsz, self.seq, self.ctx = bsz, seq, ctx
        self.n_lat = bsz * seq
        self.n_ctx = bsz * ctx
        self.rows = self.n_lat + self.n_ctx
        tm = 1024
        while seq % tm or self.n_ctx % tm:
            tm //= 2
        self.tm = tm

    def mod_spec(self, layer, k, tm):
        n_lat_tiles = self.n_lat // tm
        per_batch = self.seq // tm
        bsz = self.bsz

        def imap(i, *_):
            row = jnp.where(i < n_lat_tiles, i // per_batch, bsz)
            return ((layer * MOD_ROWS + row) * 6 + k, 0, 0)

        return pl.BlockSpec((None, 1, D_MODEL), imap)


def _softplus(x):
    return jnp.maximum(x, 0.0) + jnp.log1p(jnp.exp(-jnp.abs(x)))


PROJ_TILE = 512


def _split3(v):
    p1 = v.astype(BF16)
    r1 = v - p1.astype(F32)
    p2 = r1.astype(BF16)
    p3 = (r1 - p2.astype(F32)).astype(BF16)
    return p1, p2, p3


def _scan_mask(backward):
    li = lax.broadcasted_iota(jnp.int32, (CHUNK, CHUNK), 0)
    si = lax.broadcasted_iota(jnp.int32, (CHUNK, CHUNK), 1)
    return si >= li if backward else si <= li


def _scan_terms(backward, dt, alog):
    nh = SSD_HEADS
    a = -jnp.exp(alog)
    adt = dt * (a * LOG2E)
    mask = _scan_mask(backward)
    cum3 = _dot(mask.astype(F32).astype(BF16), jnp.concatenate(_split3(adt), axis=1))
    cum = cum3[:, :nh] + cum3[:, nh:2 * nh] + cum3[:, 2 * nh:3 * nh]
    tot = jnp.sum(adt, axis=0, keepdims=True)
    pad = jnp.zeros((CHUNK, CHUNK - nh), F32)
    tr = jnp.concatenate([cum - jnp.log2(dt), pad], axis=1).T[:nh]
    packed = jnp.concatenate([cum, dt * jnp.exp2(tot - cum), jnp.exp2(cum),
                              jnp.broadcast_to(jnp.exp2(tot), (CHUNK, nh))], axis=1)
    return packed, tr


CONV_ROWS = 256
HALO = 16


def _ssd_in_kernel(xp_ref, x_ref, xn_ref, nw_ref, sh_ref, sc_ref, w_ref, wdt_ref, dtb_ref, alog_ref,
                   cw_ref, cb_ref, z_ref, xbc_ref, pk_ref, tr_ref, *, n_lat, seq, ctx):
    def norm_mod(v):
        return (_rms(v, nw_ref[...]) * (1.0 + sc_ref[...]) + sh_ref[...]).astype(BF16)

    ab = norm_mod(x_ref[...])
    tm = ab.shape[0]
    a_ext = jnp.concatenate([norm_mod(xp_ref[...]), ab, norm_mod(xn_ref[...])], axis=0)
    for d in range(2):
        dt = _softplus(_dot(ab, wdt_ref[d]) + dtb_ref[d])
        for c in range(tm // CHUNK):
            packed, tr = _scan_terms(d == 1, dt[c * CHUNK:(c + 1) * CHUNK, :SSD_HEADS], alog_ref[d])
            pk_ref[d, c * CHUNK:(c + 1) * CHUNK, :] = packed
            tr_ref[d, c * SSD_HEADS:(c + 1) * SSD_HEADS, :] = tr
    for c in range(0, D_INNER, PROJ_TILE):
        z_ref[:, c:c + PROJ_TILE] = _dot(ab, w_ref[:, c:c + PROJ_TILE]).astype(BF16)

    row0 = pl.program_id(0) * tm
    keep = []
    for s in range(tm // CONV_ROWS):
        g0 = row0 + s * CONV_ROWS
        g1 = g0 + CONV_ROWS
        starts = jnp.where(g0 < n_lat, g0 % seq == 0, (g0 - n_lat) % ctx == 0)
        ends = jnp.where(g1 <= n_lat, g1 % seq == 0, (g1 - n_lat) % ctx == 0)
        keep.append((jnp.where(starts, 0.0, 1.0), jnp.where(ends, 0.0, 1.0)))
    n_ext = CONV_ROWS + 16
    mid = CONV_W // 2
    for c in range(0, D_CONV_CH, PROJ_TILE):
        acc = _dot(a_ext, w_ref[:, D_INNER + c:D_INNER + c + PROJ_TILE])
        cw = cw_ref[:, c:c + PROJ_TILE]
        cbias = cb_ref[:, c:c + PROJ_TILE]
        for s in range(tm // CONV_ROWS):
            base = HALO + s * CONV_ROWS
            ext = jnp.concatenate([acc[base - 8:base] * keep[s][0], acc[base:base + CONV_ROWS],
                                   acc[base + CONV_ROWS:base + CONV_ROWS + 8] * keep[s][1]], axis=0)
            y = cbias + cw[mid:mid + 1, :] * ext[8:8 + CONV_ROWS]
            for k in range(CONV_W):
                if k != mid:
                    rolled = pltpu.roll(ext, (mid - k) % n_ext, 0)
                    y = y + cw[k:k + 1, :] * rolled[8:8 + CONV_ROWS]
            xbc_ref[s * CONV_ROWS:(s + 1) * CONV_ROWS, c:c + PROJ_TILE] = (y * _sigmoid(y)).astype(BF16)


def _ssd_in_call(geom, layer, x_all, mod3, norm_w, w_zx, w_dt, dt_bias, a_log, conv_w, conv_b):
    tm = min(geom.tm, 512)
    rows = geom.rows
    tr_rows = tm // CHUNK * SSD_HEADS
    hb = tm // HALO
    n_halo = rows // HALO
    kern = functools.partial(_ssd_in_kernel, n_lat=geom.n_lat, seq=geom.seq, ctx=geom.ctx)
    return pl.pallas_call(
        kern,
        grid=(rows // tm,),
        in_specs=[
            pl.BlockSpec((HALO, D_MODEL), lambda i: (jnp.maximum(i * hb - 1, 0), 0)),
            pl.BlockSpec((tm, D_MODEL), lambda i: (i, 0)),
            pl.BlockSpec((HALO, D_MODEL), lambda i: (jnp.minimum((i + 1) * hb, n_halo - 1), 0)),
            pl.BlockSpec((1, D_MODEL), lambda i: (0, 0)),
            geom.mod_spec(layer, 0, tm),
            geom.mod_spec(layer, 1, tm),
            _resident((D_MODEL, D_ZX), lambda i: (0, 0)),
            pl.BlockSpec((2, D_MODEL, LANES), lambda i: (0, 0, 0)),
            pl.BlockSpec((2, 1, LANES), lambda i: (0, 0, 0)),
            pl.BlockSpec((2, 1, SSD_HEADS), lambda i: (0, 0, 0)),
            pl.BlockSpec((CONV_W, D_CONV_CH), lambda i: (0, 0)),
            pl.BlockSpec((1, D_CONV_CH), lambda i: (0, 0)),
        ],
        out_specs=[
            pl.BlockSpec((tm, D_INNER), lambda i: (i, 0)),
            pl.BlockSpec((tm, D_CONV_CH), lambda i: (i, 0)),
            pl.BlockSpec((2, tm, LANES), lambda i: (0, i, 0)),
            pl.BlockSpec((2, tr_rows, LANES), lambda i: (0, i, 0)),
        ],
        out_shape=[
            jax.ShapeDtypeStruct((rows, D_INNER), BF16),
            jax.ShapeDtypeStruct((rows, D_CONV_CH), BF16),
            jax.ShapeDtypeStruct((2, rows, LANES), F32),
            jax.ShapeDtypeStruct((2, rows // CHUNK * SSD_HEADS, LANES), F32),
        ],
        compiler_params=_cparams(("arbitrary",)),
        name="ssd_in_proj_conv",
    )(x_all, x_all, x_all, norm_w, mod3, mod3, w_zx, w_dt, dt_bias, a_log, conv_w, conv_b)


def _expand_heads(v, e2_ref):
    hi = v.astype(BF16)
    lo = (v - hi.astype(F32)).astype(BF16)
    return _dot(jnp.concatenate([hi, lo], axis=1), e2_ref[...])


def _scan_prep(backward, pk_ref, tr_ref, e_ref):
    nh = SSD_HEADS
    pk = pk_ref[...]
    cum = pk[:, :nh]
    ex = _expand_heads(jnp.concatenate([pk[:, nh:2 * nh], pk[:, 2 * nh:3 * nh], pk[:8, 3 * nh:]], axis=0),
                       e_ref)
    return (_scan_mask(backward), cum, tr_ref[...], ex[:CHUNK], ex[CHUNK:2 * CHUNK],
            ex[2 * CHUNK:2 * CHUNK + 1])


def _scan_group(g, prep, xs_ref, b_ref, c_ref, y_ref, h_ref):
    mask, cum, tr, wst_e, dout_e, cdec_e = prep
    lane = lax.broadcasted_iota(jnp.int32, (CHUNK, 2 * SSD_HEAD_DIM), 1)
    gw = HEADS_PER_GROUP * SSD_HEAD_DIM
    bg = b_ref[:, g * D_STATE:(g + 1) * D_STATE]
    cg = c_ref[:, g * D_STATE:(g + 1) * D_STATE]
    cb = _dot_nt(cg, bg)
    h_old = h_ref[:, g * gw:(g + 1) * gw]
    y_off = _dot(cg, h_old.astype(BF16)) * dout_e[:, g * gw:(g + 1) * gw]
    xg = xs_ref[:, g * gw:(g + 1) * gw]
    for q in range(HEADS_PER_GROUP // 2):
        ws = []
        for hh in range(2):
            h = g * HEADS_PER_GROUP + 2 * q + hh
            diff = cum[:, h:h + 1] - tr[h:h + 1, :]
            lmat = jnp.exp2(jnp.where(mask, diff, -jnp.inf))
            ws.append((cb * lmat).astype(BF16))
        xp = xg[:, q * 2 * SSD_HEAD_DIM:(q + 1) * 2 * SSD_HEAD_DIM]
        zero = jnp.zeros_like(xp)
        rhs = jnp.concatenate([jnp.where(lane < SSD_HEAD_DIM, xp, zero),
                               jnp.where(lane >= SSD_HEAD_DIM, xp, zero)], axis=0)
        yd = _dot(jnp.concatenate(ws, axis=1), rhs)
        c0 = g * gw + q * 2 * SSD_HEAD_DIM
        y_ref[:, c0:c0 + 2 * SSD_HEAD_DIM] = (
            yd + y_off[:, q * 2 * SSD_HEAD_DIM:(q + 1) * 2 * SSD_HEAD_DIM]).astype(y_ref.dtype)
    xw = (xg.astype(F32) * wst_e[:, g * gw:(g + 1) * gw]).astype(BF16)
    bgt = bg.astype(F32).T.astype(BF16)
    h_ref[:, g * gw:(g + 1) * gw] = h_old * cdec_e[:, g * gw:(g + 1) * gw] + _dot(bgt, xw)


def _ssd_scan_kernel(xf_ref, bf_ref, cf_ref, pkf_ref, trf_ref, xb_ref, bb_ref, cb_ref, pkb_ref, trb_ref,
                     e_ref, yf_ref, yb_ref, h_ref):
    @pl.when(pl.program_id(1) == 0)
    def _():
        h_ref[...] = jnp.zeros_like(h_ref)

    prep_f = _scan_prep(False, pkf_ref, trf_ref, e_ref)
    prep_b = _scan_prep(True, pkb_ref, trb_ref, e_ref)
    for g in range(SSD_GROUPS):
        _scan_group(g, prep_f, xf_ref, bf_ref, cf_ref, yf_ref, h_ref.at[0])
        _scan_group(g, prep_b, xb_ref, bb_ref, cb_ref, yb_ref, h_ref.at[1])


def _ssd_scan_call(geom, xbc, packed, tr, e_mat):
    rows = geom.rows
    nc_lat = geom.seq // CHUNK
    nc_ctx = geom.ctx // CHUNK
    lat_blocks = geom.n_lat // CHUNK

    def chunk_block(d, b, s):
        in_ctx = s < nc_ctx
        sl = s - nc_ctx
        c_ctx = nc_ctx - 1 - s if d else s
        c_lat = nc_lat - 1 - sl if d else sl
        return jnp.where(in_ctx, lat_blocks + b * nc_ctx + c_ctx, b * nc_lat + c_lat)

    xcols = D_INNER // D_BC

    def dir_specs(d):
        return [
            pl.BlockSpec((CHUNK, D_INNER), lambda b, s: (chunk_block(d, b, s), 0)),
            pl.BlockSpec((CHUNK, D_BC), lambda b, s: (chunk_block(d, b, s), xcols)),
            pl.BlockSpec((CHUNK, D_BC), lambda b, s: (chunk_block(d, b, s), xcols + 1)),
            pl.BlockSpec((None, CHUNK, LANES), lambda b, s: (d, chunk_block(d, b, s), 0)),
            pl.BlockSpec((None, SSD_HEADS, LANES), lambda b, s: (d, chunk_block(d, b, s), 0)),
        ]

    y_shape = jax.ShapeDtypeStruct((rows, D_INNER), BF16)
    return pl.pallas_call(
        _ssd_scan_kernel,
        grid=(geom.bsz, nc_ctx + nc_lat),
        in_specs=dir_specs(0) + dir_specs(1) + [
            pl.BlockSpec((2 * SSD_HEADS, D_INNER), lambda b, s: (0, 0)),
        ],
        out_specs=[pl.BlockSpec((CHUNK, D_INNER), lambda b, s: (chunk_block(0, b, s), 0)),
                   pl.BlockSpec((CHUNK, D_INNER), lambda b, s: (chunk_block(1, b, s), 0))],
        out_shape=[y_shape, y_shape],
        scratch_shapes=[pltpu.VMEM((2, D_STATE, D_INNER), F32)],
        compiler_params=_cparams(("arbitrary", "arbitrary")),
        name="ssd_scan",
    )(xbc, xbc, xbc, packed, tr, xbc, xbc, xbc, packed, tr, e_mat)


def _ssd_out_kernel(yf_ref, yb_ref, xs_ref, z_ref, dsk_ref, gw_ref, w_ref, x_ref, gate_ref, o_ref):
    y = (yf_ref[...].astype(F32) + yb_ref[...].astype(F32)
         + xs_ref[...].astype(F32) * dsk_ref[...])
    z = z_ref[...].astype(F32)
    yn = _rms(y * (z * _sigmoid(z)), gw_ref[...]).astype(BF16)
    o_ref[...] = x_ref[...] + gate_ref[...] * _dot(yn, w_ref[...])


def _ssd_out_call(geom, layer, y_f, y_b, xbc, z, d_skip_e, gnorm_w, w_out, x_all, mod3):
    tm = min(geom.tm, 512)
    rows = geom.rows
    return pl.pallas_call(
        _ssd_out_kernel,
        grid=(rows // tm,),
        in_specs=[
            pl.BlockSpec((tm, D_INNER), lambda i: (i, 0)),
            pl.BlockSpec((tm, D_INNER), lambda i: (i, 0)),
            pl.BlockSpec((tm, D_INNER), lambda i: (i, 0)),
            pl.BlockSpec((tm, D_INNER), lambda i: (i, 0)),
            pl.BlockSpec((1, D_INNER), lambda i: (0, 0)),
            pl.BlockSpec((1, D_INNER), lambda i: (0, 0)),
            _resident((D_INNER, D_MODEL), lambda i: (0, 0)),
            pl.BlockSpec((tm, D_MODEL), lambda i: (i, 0)),
            geom.mod_spec(layer, 2, tm),
        ],
        out_specs=pl.BlockSpec((tm, D_MODEL), lambda i: (i, 0)),
        out_shape=jax.ShapeDtypeStruct((rows, D_MODEL), F32),
        compiler_params=_cparams(("arbitrary",)),
        name="ssd_out_proj",
    )(y_f, y_b, xbc, z, d_skip_e, gnorm_w, w_out, x_all, mod3)


def _rope_blocks(acc, cos, sin):
    outs = []
    for c in range(acc.shape[1] // LANES):
        xb = acc[:, c * LANES:(c + 1) * LANES]
        outs.append(xb * cos + pltpu.roll(xb, LANES // 2, 1) * sin)
    return outs


def _qkv_kernel(x_ref, nw_ref, sh_ref, sc_ref, w_ref, cos_ref, sin_ref, q1_ref, q2_ref, k_ref, v_ref):
    a = _rms(x_ref[...], nw_ref[...]) * (1.0 + sc_ref[...]) + sh_ref[...]
    ab = a.astype(BF16)
    cos, sin = cos_ref[...], sin_ref[...]
    scale = DA_HEAD_DIM ** -0.5 * LOG2E
    lane = lax.broadcasted_iota(jnp.int32, (ab.shape[0], LANES), 1)
    is_map1 = (lane % (LANES // 2)) < (LANES // 4)
    for c0 in range(0, D_MODEL, PROJ_TILE):
        acc = _dot(ab, w_ref[:, c0:c0 + PROJ_TILE])
        for c, blk in enumerate(_rope_blocks(acc, cos, sin)):
            blk = blk * scale
            col = c0 + c * LANES
            q1_ref[:, col:col + LANES] = jnp.where(is_map1, blk, 0.0).astype(BF16)
            q2_ref[:, col:col + LANES] = jnp.where(is_map1, 0.0, blk).astype(BF16)
    for c0 in range(0, D_MODEL, PROJ_TILE):
        acc = _dot(ab, w_ref[:, D_MODEL + c0:D_MODEL + c0 + PROJ_TILE])
        for c, blk in enumerate(_rope_blocks(acc, cos, sin)):
            col = c0 + c * LANES
            k_ref[:, col:col + LANES] = blk.astype(BF16)
    for c0 in range(0, D_MODEL, PROJ_TILE):
        v_ref[:, c0:c0 + PROJ_TILE] = _dot(
            ab, w_ref[:, 2 * D_MODEL + c0:2 * D_MODEL + c0 + PROJ_TILE]).astype(BF16)


def _qkv_call(geom, layer, x_all, mod3, norm_w, w_qkv, cos_t, sin_t):
    tm = min(geom.tm, 512)
    rows = geom.rows
    row_out = pl.BlockSpec((tm, D_MODEL), lambda i: (i, 0))
    return pl.pallas_call(
        _qkv_kernel,
        grid=(rows // tm,),
        in_specs=[
            pl.BlockSpec((tm, D_MODEL), lambda i: (i, 0)),
            pl.BlockSpec((1, D_MODEL), lambda i: (0, 0)),
            geom.mod_spec(layer, 0, tm),
            geom.mod_spec(layer, 1, tm),
            _resident((D_MODEL, 3 * D_MODEL), lambda i: (0, 0)),
            pl.BlockSpec((tm, LANES), lambda i: (i, 0)),
            pl.BlockSpec((tm, LANES), lambda i: (i, 0)),
        ],
        out_specs=[row_out, row_out, row_out, row_out],
        out_shape=[jax.ShapeDtypeStruct((rows, D_MODEL), BF16)] * 4,
        compiler_params=_cparams(("arbitrary",)),
        name="attn_qkv_rope",
    )(x_all, norm_w, mod3, mod3, w_qkv, cos_t, sin_t)


FLASH_ROWS = 64


def _flash_kernel(lam_ref, subw_ref, q1_ref, q2_ref, *refs, seg_lens, tk, tq, aliased, lambda_init):
    n_seg = len(seg_lens)
    kv_refs = refs[:2 * n_seg]
    o_ref = refs[2 * n_seg + (1 if aliased else 0)]
    scratch = refs[2 * n_seg + (2 if aliased else 1):]

    lam_v = lam_ref[...]
    hd = DA_HEAD_DIM
    lam = (jnp.exp(jnp.sum(lam_v[:, 0:hd] * lam_v[:, hd:2 * hd], axis=1, keepdims=True))
           - jnp.exp(jnp.sum(lam_v[:, 2 * hd:3 * hd] * lam_v[:, 3 * hd:4 * hd], axis=1, keepdims=True))
           + lambda_init)

    def q_tile(qrows, st):
        s_scr, p_scr, acc_scr, m_scr, al_scr = (r.at[st] for r in scratch)
        qs = (q1_ref[qrows, :], q2_ref[qrows, :])
        acc_scr[...] = jnp.zeros_like(acc_scr)
        m_scr[...] = jnp.full_like(m_scr, -jnp.inf)

        def scores(slot, k_blk):
            w = k_blk.shape[0]
            for t in range(2):
                s_scr[slot, t, :, :w] = _dot_nt(qs[t], k_blk)

        def absorb(slot, v_blk):
            w = v_blk.shape[0]
            vaug = jnp.concatenate([v_blk, jnp.ones_like(v_blk)], axis=1)
            for t in range(2):
                for r in range(0, tq, FLASH_ROWS):
                    rows = pl.ds(r, FLASH_ROWS)
                    tiles = [s_scr[slot, t, rows, c:c + LANES] for c in range(0, w, LANES)]
                    mx = tiles[0]
                    for tl in tiles[1:]:
                        mx = jnp.maximum(mx, tl)
                    m_old = m_scr[t, rows, :]
                    m_new = jnp.maximum(m_old, jnp.max(mx, axis=1, keepdims=True))
                    for ci, tl in enumerate(tiles):
                        p_scr[t, rows, ci * LANES:(ci + 1) * LANES] = jnp.exp2(tl - m_new).astype(BF16)
                    m_scr[t, rows, :] = m_new
                    al_scr[t, rows, :] = jnp.exp2(m_old - m_new)
                alpha = al_scr[t]
                acc_scr[t] = (jnp.concatenate([alpha, alpha], axis=1) * acc_scr[t]
                              + _dot(p_scr[t, :, :w], vaug))

        if n_seg == 1:
            scores(0, kv_refs[0][...])
            absorb(0, kv_refs[1][...])
        else:
            kl_ref, vl_ref, kc_ref, vc_ref = kv_refs
            n_lat = seg_lens[0] // tk

            def k_block(i):
                return kc_ref[...] if i == n_lat else kl_ref[i * tk:(i + 1) * tk, :]

            def v_block(i):
                return vc_ref[...] if i == n_lat else vl_ref[i * tk:(i + 1) * tk, :]

            scores(0, k_block(0))
            for i in range(n_lat + 1):
                if i + 1 <= n_lat:
                    scores((i + 1) % 2, k_block(i + 1))
                absorb(i % 2, v_block(i))

        a1 = acc_scr[0]
        a2 = acc_scr[1]
        o = a1[:, :LANES] / a1[:, LANES:] - lam * (a2[:, :LANES] / a2[:, LANES:])
        o = _rms(o, subw_ref[...]) * (1.0 - lambda_init)
        o_ref[qrows, :] = o.astype(BF16)

    n_sub = q1_ref.shape[0] // tq
    if n_sub == 1:
        q_tile(pl.ds(0, tq), 0)
    else:
        def body(qi, carry):
            q_tile(pl.ds(pl.multiple_of(2 * qi * tq, tq), tq), 0)
            q_tile(pl.ds(pl.multiple_of((2 * qi + 1) * tq, tq), tq), 1)
            return carry
        lax.fori_loop(0, n_sub // 2, body, 0)


Q_SUBTILES = 4
FLASH_KEYS = 4096


def _flash_call(geom, q1, q2, k, v, lam_vec, subln_w, lambda_init, o_prev, ctx_queries, out_rows):
    bsz, seq, ctx = geom.bsz, geom.seq, geom.ctx
    tk = ctx if ctx_queries else min(FLASH_KEYS, seq)
    assert seq % tk == 0 or ctx_queries
    if ctx_queries:
        tq = min(256, ctx)
        qb = tq
        nq = ctx // qb
        q_base = geom.n_lat // qb
        seg_lens = (ctx,)
    else:
        tq = min(256, seq)
        qb = min(Q_SUBTILES * tq, seq)
        nq = seq // qb
        q_base = 0
        seg_lens = (seq, ctx)
    ctx_base = geom.n_lat // ctx
    aliased = o_prev is not None
    n_sets = 1 if qb == tq else 2
    assert (qb // tq) % n_sets == 0

    def q_map(b, h, i):
        return (q_base + b * nq + i, h)

    kv_lat = pl.BlockSpec((seq, LANES), lambda b, h, i: (b, h))
    kv_ctx = pl.BlockSpec((ctx, LANES), lambda b, h, i: (ctx_base + b, h))
    kv_specs = [kv_ctx, kv_ctx] if ctx_queries else [kv_lat, kv_lat, kv_ctx, kv_ctx]
    kv_args = [k, v] if ctx_queries else [k, v, k, v]
    kern = functools.partial(_flash_kernel, seg_lens=seg_lens, tk=tk, tq=tq, aliased=aliased,
                             lambda_init=lambda_init)
    n_in = 4 + len(kv_args)
    return pl.pallas_call(
        kern,
        grid=(bsz, DA_HEADS, nq),
        in_specs=[
            pl.BlockSpec((1, 4 * DA_HEAD_DIM), lambda b, h, i: (0, 0)),
            pl.BlockSpec((1, LANES), lambda b, h, i: (0, 0)),
            pl.BlockSpec((qb, LANES), q_map),
            pl.BlockSpec((qb, LANES), q_map),
            *kv_specs,
        ] + ([pl.BlockSpec(memory_space=pl.ANY)] if aliased else []),
        out_specs=pl.BlockSpec((qb, LANES), q_map),
        out_shape=jax.ShapeDtypeStruct((out_rows, D_MODEL), BF16),
        input_output_aliases={n_in: 0} if aliased else {},
        scratch_shapes=[pltpu.VMEM((n_sets, 2, 2, tq, tk), F32),
                        pltpu.VMEM((n_sets, 2, tq, tk), BF16),
                        pltpu.VMEM((n_sets, 2, tq, 2 * LANES), F32),
                        pltpu.VMEM((n_sets, 2, tq, LANES), F32),
                        pltpu.VMEM((n_sets, 2, tq, LANES), F32)],
        compiler_params=_cparams(("arbitrary", "arbitrary", "arbitrary")),
        name="flash_ctx" if ctx_queries else "flash_lat",
    )(lam_vec, subln_w, q1, q2, *kv_args, *([o_prev] if aliased else []))


def _attn_out_kernel(o_ref, w_ref, x_ref, gate_ref, out_ref):
    out_ref[...] = x_ref[...] + gate_ref[...] * _dot(o_ref[...], w_ref[...])


def _attn_out_call(geom, layer, o, w_o, x_all, mod3, rows):
    tm = geom.tm
    return pl.pallas_call(
        _attn_out_kernel,
        grid=(rows // tm,),
        in_specs=[
            pl.BlockSpec((tm, D_MODEL), lambda i: (i, 0)),
            _resident((D_MODEL, D_MODEL), lambda i: (0, 0)),
            pl.BlockSpec((tm, D_MODEL), lambda i: (i, 0)),
            geom.mod_spec(layer, 2, tm),
        ],
        out_specs=pl.BlockSpec((tm, D_MODEL), lambda i: (i, 0)),
        out_shape=jax.ShapeDtypeStruct((rows, D_MODEL), F32),
        compiler_params=_cparams(("arbitrary",)),
        name="attn_out_proj",
    )(o, w_o, x_all, mod3)


FFN_TILE = 256


def _ffn_kernel(x_ref, nw_ref, sh_ref, sc_ref, gate_ref, wg_ref, wu_ref, wd_ref, fw_ref,
                o_ref, h_scr, *, final_norm):
    x = x_ref[...]
    a = (_rms(x, nw_ref[...]) * (1.0 + sc_ref[...]) + sh_ref[...]).astype(BF16)
    for f in range(0, D_FF, FFN_TILE):
        g = _dot(a, wg_ref[:, f:f + FFN_TILE])
        u = _dot(a, wu_ref[:, f:f + FFN_TILE])
        h_scr[:, f:f + FFN_TILE] = (g * _sigmoid(g) * u).astype(BF16)
    out = x + gate_ref[...] * _dot(h_scr[...], wd_ref[...])
    if final_norm:
        out = _rms(out, fw_ref[...])
    o_ref[...] = out


def _ffn_call(geom, layer, x_all, mod3, norm_w, w_gate, w_up, w_down, final_w, final_norm):
    tm = min(geom.tm, 512)
    rows = geom.n_lat if final_norm else geom.rows
    kern = functools.partial(_ffn_kernel, final_norm=final_norm)
    return pl.pallas_call(
        kern,
        grid=(rows // tm,),
        in_specs=[
            pl.BlockSpec((tm, D_MODEL), lambda i: (i, 0)),
            pl.BlockSpec((1, D_MODEL), lambda i: (0, 0)),
            geom.mod_spec(layer, 3, tm),
            geom.mod_spec(layer, 4, tm),
            geom.mod_spec(layer, 5, tm),
            _resident((D_MODEL, D_FF), lambda i: (0, 0)),
            _resident((D_MODEL, D_FF), lambda i: (0, 0)),
            _resident((D_FF, D_MODEL), lambda i: (0, 0)),
            pl.BlockSpec((1, D_MODEL), lambda i: (0, 0)),
        ],
        out_specs=pl.BlockSpec((tm, D_MODEL), lambda i: (i, 0)),
        out_shape=jax.ShapeDtypeStruct((rows, D_MODEL), F32),
        scratch_shapes=[pltpu.VMEM((tm, D_FF), BF16)],
        compiler_params=_cparams(("arbitrary",)),
        name="ffn_swiglu",
    )(x_all, norm_w, mod3, mod3, mod3, w_gate, w_up, w_down, final_w)


def _rope_tables(geom):
    seq = geom.seq
    t = jnp.arange(seq)
    row = (t // GRID_W).astype(F32)
    col = (t % GRID_W).astype(F32)
    freqs = 1.0 / (ROPE_THETA ** (jnp.arange(ROPE_PAIRS, dtype=F32) / ROPE_PAIRS))
    ang = jnp.concatenate([row[:, None] * freqs, col[:, None] * freqs], axis=-1)
    cos, sin = jnp.cos(ang), jnp.sin(ang)
    cos_l = jnp.tile(cos, (geom.bsz, 4))
    sin_l = jnp.tile(jnp.concatenate([-sin, -sin, sin, sin], axis=-1), (geom.bsz, 1))
    cos_t = jnp.concatenate([cos_l, jnp.ones((geom.n_ctx, LANES), F32)], axis=0)
    sin_t = jnp.concatenate([sin_l, jnp.zeros((geom.n_ctx, LANES), F32)], axis=0)
    return cos_t, sin_t


def _qk_col_perm():
    perm = []
    quarter = DA_HEAD_DIM // 2
    for h in range(DA_HEADS):
        for half in range(2):
            for m in range(2):
                base = h * 2 * DA_HEAD_DIM + m * DA_HEAD_DIM + half * quarter
                perm.extend(range(base, base + quarter))
    return jnp.asarray(perm, dtype=jnp.int32)


def kernel(x, c, ctx, c_ctx, w_ada, b_ada, norm1_w, norm2_w, ssd_w_in, ssd_conv_w, ssd_conv_b,
           ssd_a_log, ssd_dt_bias, ssd_d, ssd_norm_w, ssd_w_out, da_w_qkv, da_w_o, da_lq1, da_lk1,
           da_lq2, da_lk2, da_subln_w, ffn_w_gate, ffn_w_up, ffn_w_down, final_norm_w):
    bsz, seq, d = x.shape
    n_ctx_tok = ctx.shape[1]
    assert d == D_MODEL and bsz + 1 <= MOD_ROWS
    assert seq % CONV_ROWS == 0 and n_ctx_tok % CONV_ROWS == 0 and seq % GRID_W == 0
    geom = _Geom(bsz, seq, n_ctx_tok)

    x_all = jnp.concatenate([x.reshape(bsz * seq, d), ctx.reshape(bsz * n_ctx_tok, d)], axis=0)
    cvec = jnp.concatenate([c, c_ctx[None, :], jnp.zeros((MOD_ROWS - bsz - 1, d), F32)], axis=0)
    mod = _ada_call(cvec, w_ada, b_ada)
    mod3 = mod.reshape(DEPTH * MOD_ROWS * 6, 1, D_MODEL)

    cos_t, sin_t = _rope_tables(geom)
    perm = _qk_col_perm()
    hq = DA_HEADS * 2 * DA_HEAD_DIM
    e_mat = (jnp.arange(D_INNER)[None, :] // SSD_HEAD_DIM == jnp.arange(SSD_HEADS)[:, None]).astype(BF16)
    e_mat = jnp.concatenate([e_mat, e_mat], axis=0)

    out = None
    for i in range(DEPTH):
        j = i // 2
        last = i == DEPTH - 1
        n1 = norm1_w[i].reshape(1, d)
        if i % 2 == 0:
            w_in = ssd_w_in[j]
            w_zx = w_in[:, :D_ZX].astype(BF16)
            w_dt = w_in[:, D_ZX:].reshape(d, 2, SSD_HEADS).transpose(1, 0, 2)
            w_dt = jnp.pad(w_dt, ((0, 0), (0, 0), (0, LANES - SSD_HEADS))).astype(BF16)
            dt_b = jnp.pad(ssd_dt_bias[j], ((0, 0), (0, LANES - SSD_HEADS))).reshape(2, 1, LANES)
            z, xbc, packed, tr = _ssd_in_call(geom, i, x_all, mod3, n1, w_zx, w_dt, dt_b,
                                              ssd_a_log[j].reshape(2, 1, SSD_HEADS), ssd_conv_w[j],
                                              ssd_conv_b[j].reshape(1, D_CONV_CH))
            y_f, y_b = _ssd_scan_call(geom, xbc, packed, tr, e_mat)
            d_skip_e = jnp.repeat(ssd_d[j], SSD_HEAD_DIM).reshape(1, D_INNER)
            x_all = _ssd_out_call(geom, i, y_f, y_b, xbc, z, d_skip_e, ssd_norm_w[j].reshape(1, D_INNER),
                                  ssd_w_out[j].astype(BF16), x_all, mod3)
        else:
            lambda_init = 0.8 - 0.6 * math.exp(-0.3 * i)
            w_qkv = da_w_qkv[j]
            w_qkv = jnp.concatenate([w_qkv[:, :hq][:, perm], w_qkv[:, hq:2 * hq][:, perm],
                                     w_qkv[:, 2 * hq:]], axis=1).astype(BF16)
            q1, q2, k, v = _qkv_call(geom, i, x_all, mod3, n1, w_qkv, cos_t, sin_t)
            lam_vec = jnp.concatenate([da_lq1[j], da_lk1[j], da_lq2[j], da_lk2[j]]).reshape(1, 4 * DA_HEAD_DIM)
            subw = da_subln_w[j].reshape(1, LANES)
            o_rows = geom.n_lat if last else geom.rows
            o = _flash_call(geom, q1, q2, k, v, lam_vec, subw, lambda_init, None, False, o_rows)
            if not last:
                o = _flash_call(geom, q1, q2, k, v, lam_vec, subw, lambda_init, o, True, o_rows)
            x_all = _attn_out_call(geom, i, o, da_w_o[j].astype(BF16), x_all, mod3, o_rows)
        res = _ffn_call(geom, i, x_all, mod3, norm2_w[i].reshape(1, d), ffn_w_gate[i].astype(BF16),
                        ffn_w_up[i].astype(BF16), ffn_w_down[i].astype(BF16),
                        final_norm_w.reshape(1, d), final_norm=last)
        if last:
            out = res
        else:
            x_all = res
    return out.reshape(bsz, seq, d)
```

```python
import functools
import math

import jax
import jax.numpy as jnp
from jax import lax
from jax.experimental import pallas as pl
from jax.experimental.pallas import tpu as pltpu

F32 = jnp.float32
BF16 = jnp.bfloat16

D_MODEL = 1024
DEPTH = 4
GRID_W = 64
EPS = 1e-6

D_INNER = 2 * D_MODEL
SSD_HEAD_DIM = 64
SSD_HEADS = D_INNER // SSD_HEAD_DIM
SSD_GROUPS = 4
HEADS_PER_GROUP = SSD_HEADS // SSD_GROUPS
D_STATE = 128
CONV_W = 5
CHUNK = 128
D_BC = SSD_GROUPS * D_STATE
D_CONV_CH = D_INNER + 2 * D_BC
D_ZX = D_INNER + D_CONV_CH

DA_HEADS = D_MODEL // 128
DA_HEAD_DIM = 64
ROPE_THETA = 10000.0
ROPE_PAIRS = DA_HEAD_DIM // 4

D_FF = -(-8 * D_MODEL // (3 * 256)) * 256

LANES = 128
LOG2E = math.log2(math.e)
MOD_ROWS = 16
VMEM_LIMIT = 56 * 1024 * 1024


def _cparams(sem):
    return pltpu.CompilerParams(dimension_semantics=sem, vmem_limit_bytes=VMEM_LIMIT)


def _dot(a, b):
    return jnp.dot(a, b, preferred_element_type=F32)


def _dot_nt(a, b):
    return lax.dot_general(a, b, (((1,), (1,)), ((), ())), preferred_element_type=F32)


def _sigmoid(x):
    return 1.0 / (1.0 + jnp.exp(-x))


def _rms(xf, w):
    return xf * lax.rsqrt(jnp.mean(xf * xf, axis=-1, keepdims=True) + EPS) * w


def _resident(shape, index_map):
    return pl.BlockSpec(shape, index_map, pipeline_mode=pl.Buffered(1))


def _ada_kernel(c_ref, w_ref, b_ref, o_ref):
    cv = c_ref[...]
    s = (cv * _sigmoid(cv)).astype(BF16)
    o_ref[...] = _dot(s, w_ref[...].astype(BF16)) + b_ref[...]


def _ada_call(cvec, w_ada, b_ada):
    depth, d, n = w_ada.shape
    tn = 2048
    return pl.pallas_call(
        _ada_kernel,
        grid=(depth, n // tn),
        in_specs=[
            pl.BlockSpec((MOD_ROWS, d), lambda l, j: (0, 0)),
            pl.BlockSpec((None, d, tn), lambda l, j: (l, 0, j)),
            pl.BlockSpec((None, 1, tn), lambda l, j: (l, 0, j)),
        ],
        out_specs=pl.BlockSpec((None, MOD_ROWS, tn), lambda l, j: (l, 0, j)),
        out_shape=jax.ShapeDtypeStruct((depth, MOD_ROWS, n), F32),
        compiler_params=_cparams(("arbitrary", "arbitrary")),
        name="adaln",
    )(cvec, w_ada, b_ada.reshape(depth, 1, n))


class _Geom:
    def __init__(self, bsz, seq, ctx):
        self.bsz, self.seq, self.ctx = bsz, seq, ctx
        self.n_lat = bsz * seq
        self.n_ctx = bsz * ctx
        self.rows = self.n_lat + self.n_ctx
        tm = 1024
        while seq % tm or self.n_ctx % tm:
            tm //= 2
        self.tm = tm

    def mod_spec(self, layer, k, tm):
        n_lat_tiles = self.n_lat // tm
        per_batch = self.seq // tm
        bsz = self.bsz

        def imap(i, *_):
            row = jnp.where(i < n_lat_tiles, i // per_batch, bsz)
            return ((layer * MOD_ROWS + row) * 6 + k, 0, 0)

        return pl.BlockSpec((None, 1, D_MODEL), imap)


def _softplus(x):
    return jnp.maximum(x, 0.0) + jnp.log1p(jnp.exp(-jnp.abs(x)))


PROJ_TILE = 512


def _split3(v):
    p1 = v.astype(BF16)
    r1 = v - p1.astype(F32)
    p2 = r1.astype(BF16)
    p3 = (r1 - p2.astype(F32)).astype(BF16)
    return p1, p2, p3


def _scan_mask(backward):
    li = lax.broadcasted_iota(jnp.int32, (CHUNK, CHUNK), 0)
    si = lax.broadcasted_iota(jnp.int32, (CHUNK, CHUNK), 1)
    return si >= li if backward else si <= li


def _scan_terms(backward, dt, alog):
    nh = SSD_HEADS
    a = -jnp.exp(alog)
    adt = dt * (a * LOG2E)
    mask = _scan_mask(backward)
    cum3 = _dot(mask.astype(F32).astype(BF16), jnp.concatenate(_split3(adt), axis=1))
    cum = cum3[:, :nh] + cum3[:, nh:2 * nh] + cum3[:, 2 * nh:3 * nh]
    tot = jnp.sum(adt, axis=0, keepdims=True)
    pad = jnp.zeros((CHUNK, CHUNK - nh), F32)
    tr = jnp.concatenate([cum - jnp.log2(dt), pad], axis=1).T[:nh]
    packed = jnp.concatenate([cum, dt * jnp.exp2(tot - cum), jnp.exp2(cum),
                              jnp.broadcast_to(jnp.exp2(tot), (CHUNK, nh))], axis=1)
    return packed, tr


CONV_ROWS = 256
HALO = 16


def _ssd_in_kernel(xp_ref, x_ref, xn_ref, nw_ref, sh_ref, sc_ref, w_ref, wdt_ref, dtb_ref, alog_ref,
                   cw_ref, cb_ref, z_ref, xbc_ref, pk_ref, tr_ref, *, n_lat, seq, ctx):
    def norm_mod(v):
        return (_rms(v, nw_ref[...]) * (1.0 + sc_ref[...]) + sh_ref[...]).astype(BF16)

    ab = norm_mod(x_ref[...])
    tm = ab.shape[0]
    a_ext = jnp.concatenate([norm_mod(xp_ref[...]), ab, norm_mod(xn_ref[...])], axis=0)
    for d in range(2):
        dt = _softplus(_dot(ab, wdt_ref[d]) + dtb_ref[d])
        for c in range(tm // CHUNK):
            packed, tr = _scan_terms(d == 1, dt[c * CHUNK:(c + 1) * CHUNK, :SSD_HEADS], alog_ref[d])
            pk_ref[d, c * CHUNK:(c + 1) * CHUNK, :] = packed
            tr_ref[d, c * SSD_HEADS:(c + 1) * SSD_HEADS, :] = tr
    for c in range(0, D_INNER, PROJ_TILE):
        z_ref[:, c:c + PROJ_TILE] = _dot(ab, w_ref[:, c:c + PROJ_TILE]).astype(BF16)

    row0 = pl.program_id(0) * tm
    keep = []
    for s in range(tm // CONV_ROWS):
        g0 = row0 + s * CONV_ROWS
        g1 = g0 + CONV_ROWS
        starts = jnp.where(g0 < n_lat, g0 % seq == 0, (g0 - n_lat) % ctx == 0)
        ends = jnp.where(g1 <= n_lat, g1 % seq == 0, (g1 - n_lat) % ctx == 0)
        keep.append((jnp.where(starts, 0.0, 1.0), jnp.where(ends, 0.0, 1.0)))
    n_ext = CONV_ROWS + 16
    mid = CONV_W // 2
    for c in range(0, D_CONV_CH, PROJ_TILE):
        acc = _dot(a_ext, w_ref[:, D_INNER + c:D_INNER + c + PROJ_TILE])
        cw = cw_ref[:, c:c + PROJ_TILE]
        cbias = cb_ref[:, c:c + PROJ_TILE]
        for s in range(tm // CONV_ROWS):
            base = HALO + s * CONV_ROWS
            ext = jnp.concatenate([acc[base - 8:base] * keep[s][0], acc[base:base + CONV_ROWS],
                                   acc[base + CONV_ROWS:base + CONV_ROWS + 8] * keep[s][1]], axis=0)
            y = cbias + cw[mid:mid + 1, :] * ext[8:8 + CONV_ROWS]
            for k in range(CONV_W):
                if k != mid:
                    rolled = pltpu.roll(ext, (mid - k) % n_ext, 0)
                    y = y + cw[k:k + 1, :] * rolled[8:8 + CONV_ROWS]
            xbc_ref[s * CONV_ROWS:(s + 1) * CONV_ROWS, c:c + PROJ_TILE] = (y * _sigmoid(y)).astype(BF16)


def _ssd_in_call(geom, layer, x_all, mod3, norm_w, w_zx, w_dt, dt_bias, a_log, conv_w, conv_b):
    tm = min(geom.tm, 512)
    rows = geom.rows
    tr_rows = tm // CHUNK * SSD_HEADS
    hb = tm // HALO
    n_halo = rows // HALO
    kern = functools.partial(_ssd_in_kernel, n_lat=geom.n_lat, seq=geom.seq, ctx=geom.ctx)
    return pl.pallas_call(
        kern,
        grid=(rows // tm,),
        in_specs=[
            pl.BlockSpec((HALO, D_MODEL), lambda i: (jnp.maximum(i * hb - 1, 0), 0)),
            pl.BlockSpec((tm, D_MODEL), lambda i: (i, 0)),
            pl.BlockSpec((HALO, D_MODEL), lambda i: (jnp.minimum((i + 1) * hb, n_halo - 1), 0)),
            pl.BlockSpec((1, D_MODEL), lambda i: (0, 0)),
            geom.mod_spec(layer, 0, tm),
            geom.mod_spec(layer, 1, tm),
            _resident((D_MODEL, D_ZX), lambda i: (0, 0)),
            pl.BlockSpec((2, D_MODEL, LANES), lambda i: (0, 0, 0)),
            pl.BlockSpec((2, 1, LANES), lambda i: (0, 0, 0)),
            pl.BlockSpec((2, 1, SSD_HEADS), lambda i: (0, 0, 0)),
            pl.BlockSpec((CONV_W, D_CONV_CH), lambda i: (0, 0)),
            pl.BlockSpec((1, D_CONV_CH), lambda i: (0, 0)),
        ],
        out_specs=[
            pl.BlockSpec((tm, D_INNER), lambda i: (i, 0)),
            pl.BlockSpec((tm, D_CONV_CH), lambda i: (i, 0)),
            pl.BlockSpec((2, tm, LANES), lambda i: (0, i, 0)),
            pl.BlockSpec((2, tr_rows, LANES), lambda i: (0, i, 0)),
        ],
        out_shape=[
            jax.ShapeDtypeStruct((rows, D_INNER), BF16),
            jax.ShapeDtypeStruct((rows, D_CONV_CH), BF16),
            jax.ShapeDtypeStruct((2, rows, LANES), F32),
            jax.ShapeDtypeStruct((2, rows // CHUNK * SSD_HEADS, LANES), F32),
        ],
        compiler_params=_cparams(("arbitrary",)),
        name="ssd_in_proj_conv",
    )(x_all, x_all, x_all, norm_w, mod3, mod3, w_zx, w_dt, dt_bias, a_log, conv_w, conv_b)


def _expand_heads(v, e2_ref):
    hi = v.astype(BF16)
    lo = (v - hi.astype(F32)).astype(BF16)
    return _dot(jnp.concatenate([hi, lo], axis=1), e2_ref[...])


def _scan_prep(backward, pk_ref, tr_ref, e_ref):
    nh = SSD_HEADS
    pk = pk_ref[...]
    cum = pk[:, :nh]
    ex = _expand_heads(jnp.concatenate([pk[:, nh:2 * nh], pk[:, 2 * nh:3 * nh], pk[:8, 3 * nh:]], axis=0),
                       e_ref)
    return (_scan_mask(backward), cum, tr_ref[...], ex[:CHUNK], ex[CHUNK:2 * CHUNK],
            ex[2 * CHUNK:2 * CHUNK + 1])


def _scan_group(g, prep, xs_ref, b_ref, c_ref, y_ref, h_ref):
    mask, cum, tr, wst_e, dout_e, cdec_e = prep
    lane = lax.broadcasted_iota(jnp.int32, (CHUNK, 2 * SSD_HEAD_DIM), 1)
    gw = HEADS_PER_GROUP * SSD_HEAD_DIM
    bg = b_ref[:, g * D_STATE:(g + 1) * D_STATE]
    cg = c_ref[:, g * D_STATE:(g + 1) * D_STATE]
    cb = _dot_nt(cg, bg)
    h_old = h_ref[:, g * gw:(g + 1) * gw]
    y_off = _dot(cg, h_old.astype(BF16)) * dout_e[:, g * gw:(g + 1) * gw]
    xg = xs_ref[:, g * gw:(g + 1) * gw]
    for q in range(HEADS_PER_GROUP // 2):
        ws = []
        for hh in range(2):
            h = g * HEADS_PER_GROUP + 2 * q + hh
            diff = cum[:, h:h + 1] - tr[h:h + 1, :]
            lmat = jnp.exp2(jnp.where(mask, diff, -jnp.inf))
            ws.append((cb * lmat).astype(BF16))
        xp = xg[:, q * 2 * SSD_HEAD_DIM:(q + 1) * 2 * SSD_HEAD_DIM]
        zero = jnp.zeros_like(xp)
        rhs = jnp.concatenate([jnp.where(lane < SSD_HEAD_DIM, xp, zero),
                               jnp.where(lane >= SSD_HEAD_DIM, xp, zero)], axis=0)
        yd = _dot(jnp.concatenate(ws, axis=1), rhs)
        c0 = g * gw + q * 2 * SSD_HEAD_DIM
        y_ref[:, c0:c0 + 2 * SSD_HEAD_DIM] = (
            yd + y_off[:, q * 2 * SSD_HEAD_DIM:(q + 1) * 2 * SSD_HEAD_DIM]).astype(y_ref.dtype)
    xw = (xg.astype(F32) * wst_e[:, g * gw:(g + 1) * gw]).astype(BF16)
    bgt = bg.astype(F32).T.astype(BF16)
    h_ref[:, g * gw:(g + 1) * gw] = h_old * cdec_e[:, g * gw:(g + 1) * gw] + _dot(bgt, xw)


def _ssd_scan_kernel(xf_ref, bf_ref, cf_ref, pkf_ref, trf_ref, xb_ref, bb_ref, cb_ref, pkb_ref, trb_ref,
                     e_ref, yf_ref, yb_ref, h_ref):
    @pl.when(pl.program_id(1) == 0)
    def _():
        h_ref[...] = jnp.zeros_like(h_ref)

    prep_f = _scan_prep(False, pkf_ref, trf_ref, e_ref)
    prep_b = _scan_prep(True, pkb_ref, trb_ref, e_ref)
    for g in range(SSD_GROUPS):
        _scan_group(g, prep_f, xf_ref, bf_ref, cf_ref, yf_ref, h_ref.at[0])
        _scan_group(g, prep_b, xb_ref, bb_ref, cb_ref, yb_ref, h_ref.at[1])


def _ssd_scan_call(geom, xbc, packed, tr, e_mat):
    rows = geom.rows
    nc_lat = geom.seq // CHUNK
    nc_ctx = geom.ctx // CHUNK
    lat_blocks = geom.n_lat // CHUNK

    def chunk_block(d, b, s):
        in_ctx = s < nc_ctx
        sl = s - nc_ctx
        c_ctx = nc_ctx - 1 - s if d else s
        c_lat = nc_lat - 1 - sl if d else sl
        return jnp.where(in_ctx, lat_blocks + b * nc_ctx + c_ctx, b * nc_lat + c_lat)

    xcols = D_INNER // D_BC

    def dir_specs(d):
        return [
            pl.BlockSpec((CHUNK, D_INNER), lambda b, s: (chunk_block(d, b, s), 0)),
            pl.BlockSpec((CHUNK, D_BC), lambda b, s: (chunk_block(d, b, s), xcols)),
            pl.BlockSpec((CHUNK, D_BC), lambda b, s: (chunk_block(d, b, s), xcols + 1)),
            pl.BlockSpec((None, CHUNK, LANES), lambda b, s: (d, chunk_block(d, b, s), 0)),
            pl.BlockSpec((None, SSD_HEADS, LANES), lambda b, s: (d, chunk_block(d, b, s), 0)),
        ]

    y_shape = jax.ShapeDtypeStruct((rows, D_INNER), BF16)
    return pl.pallas_call(
        _ssd_scan_kernel,
        grid=(geom.bsz, nc_ctx + nc_lat),
        in_specs=dir_specs(0) + dir_specs(1) + [
            pl.BlockSpec((2 * SSD_HEADS, D_INNER), lambda b, s: (0, 0)),
        ],
        out_specs=[pl.BlockSpec((CHUNK, D_INNER), lambda b, s: (chunk_block(0, b, s), 0)),
                   pl.BlockSpec((CHUNK, D_INNER), lambda b, s: (chunk_block(1, b, s), 0))],
        out_shape=[y_shape, y_shape],
        scratch_shapes=[pltpu.VMEM((2, D_STATE, D_INNER), F32)],
        compiler_params=_cparams(("arbitrary", "arbitrary")),
        name="ssd_scan",
    )(xbc, xbc, xbc, packed, tr, xbc, xbc, xbc, packed, tr, e_mat)


def _ssd_out_kernel(yf_ref, yb_ref, xs_ref, z_ref, dsk_ref, gw_ref, w_ref, x_ref, gate_ref, o_ref):
    y = (yf_ref[...].astype(F32) + yb_ref[...].astype(F32)
         + xs_ref[...].astype(F32) * dsk_ref[...])
    z = z_ref[...].astype(F32)
    yn = _rms(y * (z * _sigmoid(z)), gw_ref[...]).astype(BF16)
    o_ref[...] = x_ref[...] + gate_ref[...] * _dot(yn, w_ref[...])


def _ssd_out_call(geom, layer, y_f, y_b, xbc, z, d_skip_e, gnorm_w, w_out, x_all, mod3):
    tm = min(geom.tm, 512)
    rows = geom.rows
    return pl.pallas_call(
        _ssd_out_kernel,
        grid=(rows // tm,),
        in_specs=[
            pl.BlockSpec((tm, D_INNER), lambda i: (i, 0)),
            pl.BlockSpec((tm, D_INNER), lambda i: (i, 0)),
            pl.BlockSpec((tm, D_INNER), lambda i: (i, 0)),
            pl.BlockSpec((tm, D_INNER), lambda i: (i, 0)),
            pl.BlockSpec((1, D_INNER), lambda i: (0, 0)),
            pl.BlockSpec((1, D_INNER), lambda i: (0, 0)),
            _resident((None, D_INNER, D_MODEL), lambda i: (layer // 2, 0, 0)),
            pl.BlockSpec((tm, D_MODEL), lambda i: (i, 0)),
            geom.mod_spec(layer, 2, tm),
        ],
        out_specs=pl.BlockSpec((tm, D_MODEL), lambda i: (i, 0)),
        out_shape=jax.ShapeDtypeStruct((rows, D_MODEL), F32),
        compiler_params=_cparams(("arbitrary",)),
        name="ssd_out_proj",
    )(y_f, y_b, xbc, z, d_skip_e, gnorm_w, w_out, x_all, mod3)


def _rope_blocks(acc, cos, sin):
    outs = []
    for c in range(acc.shape[1] // LANES):
        xb = acc[:, c * LANES:(c + 1) * LANES]
        outs.append(xb * cos + pltpu.roll(xb, LANES // 2, 1) * sin)
    return outs


def _qkv_kernel(x_ref, nw_ref, sh_ref, sc_ref, w_ref, cos_ref, sin_ref, q1_ref, q2_ref, k_ref, v_ref):
    a = _rms(x_ref[...], nw_ref[...]) * (1.0 + sc_ref[...]) + sh_ref[...]
    ab = a.astype(BF16)
    cos, sin = cos_ref[...], sin_ref[...]
    scale = DA_HEAD_DIM ** -0.5 * LOG2E
    lane = lax.broadcasted_iota(jnp.int32, (ab.shape[0], LANES), 1)
    is_map1 = (lane % (LANES // 2)) < (LANES // 4)
    for c0 in range(0, D_MODEL, PROJ_TILE):
        acc = _dot(ab, w_ref[:, c0:c0 + PROJ_TILE])
        for c, blk in enumerate(_rope_blocks(acc, cos, sin)):
            blk = blk * scale
            col = c0 + c * LANES
            q1_ref[:, col:col + LANES] = jnp.where(is_map1, blk, 0.0).astype(BF16)
            q2_ref[:, col:col + LANES] = jnp.where(is_map1, 0.0, blk).astype(BF16)
    for c0 in range(0, D_MODEL, PROJ_TILE):
        acc = _dot(ab, w_ref[:, D_MODEL + c0:D_MODEL + c0 + PROJ_TILE])
        for c, blk in enumerate(_rope_blocks(acc, cos, sin)):
            col = c0 + c * LANES
            k_ref[:, col:col + LANES] = blk.astype(BF16)
    for c0 in range(0, D_MODEL, PROJ_TILE):
        v_ref[:, c0:c0 + PROJ_TILE] = _dot(
            ab, w_ref[:, 2 * D_MODEL + c0:2 * D_MODEL + c0 + PROJ_TILE]).astype(BF16)


def _qkv_call(geom, layer, x_all, mod3, norm_w, w_qkv, cos_t, sin_t):
    tm = min(geom.tm, 512)
    rows = geom.rows
    row_out = pl.BlockSpec((tm, D_MODEL), lambda i: (i, 0))
    return pl.pallas_call(
        _qkv_kernel,
        grid=(rows // tm,),
        in_specs=[
            pl.BlockSpec((tm, D_MODEL), lambda i: (i, 0)),
            pl.BlockSpec((1, D_MODEL), lambda i: (0, 0)),
            geom.mod_spec(layer, 0, tm),
            geom.mod_spec(layer, 1, tm),
            _resident((D_MODEL, 3 * D_MODEL), lambda i: (0, 0)),
            pl.BlockSpec((tm, LANES), lambda i: (i, 0)),
            pl.BlockSpec((tm, LANES), lambda i: (i, 0)),
        ],
        out_specs=[row_out, row_out, row_out, row_out],
        out_shape=[jax.ShapeDtypeStruct((rows, D_MODEL), BF16)] * 4,
        compiler_params=_cparams(("arbitrary",)),
        name="attn_qkv_rope",
    )(x_all, norm_w, mod3, mod3, w_qkv, cos_t, sin_t)


FLASH_ROWS = 64


def _flash_kernel(lam_ref, subw_ref, q1_ref, q2_ref, *refs, seg_lens, tk, tq, aliased, lambda_init):
    n_seg = len(seg_lens)
    kv_refs = refs[:2 * n_seg]
    o_ref = refs[2 * n_seg + (1 if aliased else 0)]
    scratch = refs[2 * n_seg + (2 if aliased else 1):]

    lam_v = lam_ref[...]
    hd = DA_HEAD_DIM
    lam = (jnp.exp(jnp.sum(lam_v[:, 0:hd] * lam_v[:, hd:2 * hd], axis=1, keepdims=True))
           - jnp.exp(jnp.sum(lam_v[:, 2 * hd:3 * hd] * lam_v[:, 3 * hd:4 * hd], axis=1, keepdims=True))
           + lambda_init)

    def q_tile(qrows, st):
        s_scr, p_scr, acc_scr, m_scr, al_scr = (r.at[st] for r in scratch)
        qs = (q1_ref[qrows, :], q2_ref[qrows, :])
        acc_scr[...] = jnp.zeros_like(acc_scr)
        m_scr[...] = jnp.full_like(m_scr, -jnp.inf)

        def scores(slot, k_blk):
            w = k_blk.shape[0]
            for t in range(2):
                s_scr[slot, t, :, :w] = _dot_nt(qs[t], k_blk)

        def absorb(slot, v_blk):
            w = v_blk.shape[0]
            vaug = jnp.concatenate([v_blk, jnp.ones_like(v_blk)], axis=1)
            for t in range(2):
                for r in range(0, tq, FLASH_ROWS):
                    rows = pl.ds(r, FLASH_ROWS)
                    tiles = [s_scr[slot, t, rows, c:c + LANES] for c in range(0, w, LANES)]
                    mx = tiles[0]
                    for tl in tiles[1:]:
                        mx = jnp.maximum(mx, tl)
                    m_old = m_scr[t, rows, :]
                    m_new = jnp.maximum(m_old, jnp.max(mx, axis=1, keepdims=True))
                    for ci, tl in enumerate(tiles):
                        p_scr[t, rows, ci * LANES:(ci + 1) * LANES] = jnp.exp2(tl - m_new).astype(BF16)
                    m_scr[t, rows, :] = m_new
                    al_scr[t, rows, :] = jnp.exp2(m_old - m_new)
                alpha = al_scr[t]
                acc_scr[t] = (jnp.concatenate([alpha, alpha], axis=1) * acc_scr[t]
                              + _dot(p_scr[t, :, :w], vaug))

        if n_seg == 1:
            scores(0, kv_refs[0][...])
            absorb(0, kv_refs[1][...])
        else:
            kl_ref, vl_ref, kc_ref, vc_ref = kv_refs
            n_lat = seg_lens[0] // tk

            def k_block(i):
                return kc_ref[...] if i == n_lat else kl_ref[i * tk:(i + 1) * tk, :]

            def v_block(i):
                return vc_ref[...] if i == n_lat else vl_ref[i * tk:(i + 1) * tk, :]

            scores(0, k_block(0))
            for i in range(n_lat + 1):
                if i + 1 <= n_lat:
                    scores((i + 1) % 2, k_block(i + 1))
                absorb(i % 2, v_block(i))

        a1 = acc_scr[0]
        a2 = acc_scr[1]
        o = a1[:, :LANES] / a1[:, LANES:] - lam * (a2[:, :LANES] / a2[:, LANES:])
        o = _rms(o, subw_ref[...]) * (1.0 - lambda_init)
        o_ref[qrows, :] = o.astype(BF16)

    n_sub = q1_ref.shape[0] // tq
    if n_sub == 1:
        q_tile(pl.ds(0, tq), 0)
    else:
        def body(qi, carry):
            q_tile(pl.ds(pl.multiple_of(2 * qi * tq, tq), tq), 0)
            q_tile(pl.ds(pl.multiple_of((2 * qi + 1) * tq, tq), tq), 1)
            return carry
        lax.fori_loop(0, n_sub // 2, body, 0)


Q_SUBTILES = 4
FLASH_KEYS = 4096


def _flash_call(geom, q1, q2, k, v, lam_vec, subln_w, lambda_init, o_prev, ctx_queries, out_rows):
    bsz, seq, ctx = geom.bsz, geom.seq, geom.ctx
    tk = ctx if ctx_queries else min(FLASH_KEYS, seq)
    assert seq % tk == 0 or ctx_queries
    if ctx_queries:
        tq = min(256, ctx)
        qb = tq
        nq = ctx // qb
        q_base = geom.n_lat // qb
        seg_lens = (ctx,)
    else:
        tq = min(256, seq)
        qb = min(Q_SUBTILES * tq, seq)
        nq = seq // qb
        q_base = 0
        seg_lens = (seq, ctx)
    ctx_base = geom.n_lat // ctx
    aliased = o_prev is not None
    n_sets = 1 if qb == tq else 2
    assert (qb // tq) % n_sets == 0

    def q_map(b, h, i):
        return (q_base + b * nq + i, h)

    kv_lat = pl.BlockSpec((seq, LANES), lambda b, h, i: (b, h))
    kv_ctx = pl.BlockSpec((ctx, LANES), lambda b, h, i: (ctx_base + b, h))
    kv_specs = [kv_ctx, kv_ctx] if ctx_queries else [kv_lat, kv_lat, kv_ctx, kv_ctx]
    kv_args = [k, v] if ctx_queries else [k, v, k, v]
    kern = functools.partial(_flash_kernel, seg_lens=seg_lens, tk=tk, tq=tq, aliased=aliased,
                             lambda_init=lambda_init)
    n_in = 4 + len(kv_args)
    return pl.pallas_call(
        kern,
        grid=(bsz, DA_HEADS, nq),
        in_specs=[
            pl.BlockSpec((1, 4 * DA_HEAD_DIM), lambda b, h, i: (0, 0)),
            pl.BlockSpec((1, LANES), lambda b, h, i: (0, 0)),
            pl.BlockSpec((qb, LANES), q_map),
            pl.BlockSpec((qb, LANES), q_map),
            *kv_specs,
        ] + ([pl.BlockSpec(memory_space=pl.ANY)] if aliased else []),
        out_specs=pl.BlockSpec((qb, LANES), q_map),
        out_shape=jax.ShapeDtypeStruct((out_rows, D_MODEL), BF16),
        input_output_aliases={n_in: 0} if aliased else {},
        scratch_shapes=[pltpu.VMEM((n_sets, 2, 2, tq, tk), F32),
                        pltpu.VMEM((n_sets, 2, tq, tk), BF16),
                        pltpu.VMEM((n_sets, 2, tq, 2 * LANES), F32),
                        pltpu.VMEM((n_sets, 2, tq, LANES), F32),
                        pltpu.VMEM((n_sets, 2, tq, LANES), F32)],
        compiler_params=_cparams(("arbitrary", "arbitrary", "arbitrary")),
        name="flash_ctx" if ctx_queries else "flash_lat",
    )(lam_vec, subln_w, q1, q2, *kv_args, *([o_prev] if aliased else []))


def _attn_out_kernel(o_ref, w_ref, x_ref, gate_ref, out_ref):
    out_ref[...] = x_ref[...] + gate_ref[...] * _dot(o_ref[...], w_ref[...])


def _attn_out_call(geom, layer, o, w_o, x_all, mod3, rows):
    tm = geom.tm
    return pl.pallas_call(
        _attn_out_kernel,
        grid=(rows // tm,),
        in_specs=[
            pl.BlockSpec((tm, D_MODEL), lambda i: (i, 0)),
            _resident((None, D_MODEL, D_MODEL), lambda i: (layer // 2, 0, 0)),
            pl.BlockSpec((tm, D_MODEL), lambda i: (i, 0)),
            geom.mod_spec(layer, 2, tm),
        ],
        out_specs=pl.BlockSpec((tm, D_MODEL), lambda i: (i, 0)),
        out_shape=jax.ShapeDtypeStruct((rows, D_MODEL), F32),
        compiler_params=_cparams(("arbitrary",)),
        name="attn_out_proj",
    )(o, w_o, x_all, mod3)


FFN_TILE = 256


def _ffn_kernel(x_ref, nw_ref, sh_ref, sc_ref, gate_ref, wg_ref, wu_ref, wd_ref, fw_ref,
                o_ref, h_scr, *, final_norm):
    x = x_ref[...]
    a = (_rms(x, nw_ref[...]) * (1.0 + sc_ref[...]) + sh_ref[...]).astype(BF16)
    for f in range(0, D_FF, FFN_TILE):
        g = _dot(a, wg_ref[:, f:f + FFN_TILE])
        u = _dot(a, wu_ref[:, f:f + FFN_TILE])
        h_scr[:, f:f + FFN_TILE] = (g * _sigmoid(g) * u).astype(BF16)
    out = x + gate_ref[...] * _dot(h_scr[...], wd_ref[...])
    if final_norm:
        out = _rms(out, fw_ref[...])
    o_ref[...] = out


def _ffn_call(geom, layer, x_all, mod3, norm_w, w_gate, w_up, w_down, final_w, final_norm):
    tm = min(geom.tm, 512)
    rows = geom.n_lat if final_norm else geom.rows
    kern = functools.partial(_ffn_kernel, final_norm=final_norm)
    return pl.pallas_call(
        kern,
        grid=(rows // tm,),
        in_specs=[
            pl.BlockSpec((tm, D_MODEL), lambda i: (i, 0)),
            pl.BlockSpec((1, D_MODEL), lambda i: (0, 0)),
            geom.mod_spec(layer, 3, tm),
            geom.mod_spec(layer, 4, tm),
            geom.mod_spec(layer, 5, tm),
            _resident((None, D_MODEL, D_FF), lambda i: (layer, 0, 0)),
            _resident((None, D_MODEL, D_FF), lambda i: (layer, 0, 0)),
            _resident((None, D_FF, D_MODEL), lambda i: (layer, 0, 0)),
            pl.BlockSpec((1, D_MODEL), lambda i: (0, 0)),
        ],
        out_specs=pl.BlockSpec((tm, D_MODEL), lambda i: (i, 0)),
        out_shape=jax.ShapeDtypeStruct((rows, D_MODEL), F32),
        scratch_shapes=[pltpu.VMEM((tm, D_FF), BF16)],
        compiler_params=_cparams(("arbitrary",)),
        name="ffn_swiglu",
    )(x_all, norm_w, mod3, mod3, mod3, w_gate, w_up, w_down, final_w)


def _rope_tables(geom):
    seq = geom.seq
    t = jnp.arange(seq)
    row = (t // GRID_W).astype(F32)
    col = (t % GRID_W).astype(F32)
    freqs = 1.0 / (ROPE_THETA ** (jnp.arange(ROPE_PAIRS, dtype=F32) / ROPE_PAIRS))
    ang = jnp.concatenate([row[:, None] * freqs, col[:, None] * freqs], axis=-1)
    cos, sin = jnp.cos(ang), jnp.sin(ang)
    cos_l = jnp.tile(cos, (geom.bsz, 4))
    sin_l = jnp.tile(jnp.concatenate([-sin, -sin, sin, sin], axis=-1), (geom.bsz, 1))
    cos_t = jnp.concatenate([cos_l, jnp.ones((geom.n_ctx, LANES), F32)], axis=0)
    sin_t = jnp.concatenate([sin_l, jnp.zeros((geom.n_ctx, LANES), F32)], axis=0)
    return cos_t, sin_t


def _qk_col_perm():
    perm = []
    quarter = DA_HEAD_DIM // 2
    for h in range(DA_HEADS):
        for half in range(2):
            for m in range(2):
                base = h * 2 * DA_HEAD_DIM + m * DA_HEAD_DIM + half * quarter
                perm.extend(range(base, base + quarter))
    return jnp.asarray(perm, dtype=jnp.int32)


def kernel(x, c, ctx, c_ctx, w_ada, b_ada, norm1_w, norm2_w, ssd_w_in, ssd_conv_w, ssd_conv_b,
           ssd_a_log, ssd_dt_bias, ssd_d, ssd_norm_w, ssd_w_out, da_w_qkv, da_w_o, da_lq1, da_lk1,
           da_lq2, da_lk2, da_subln_w, ffn_w_gate, ffn_w_up, ffn_w_down, final_norm_w):
    bsz, seq, d = x.shape
    n_ctx_tok = ctx.shape[1]
    assert d == D_MODEL and bsz + 1 <= MOD_ROWS
    assert seq % CONV_ROWS == 0 and n_ctx_tok % CONV_ROWS == 0 and seq % GRID_W == 0
    geom = _Geom(bsz, seq, n_ctx_tok)

    x_all = jnp.concatenate([x.reshape(bsz * seq, d), ctx.reshape(bsz * n_ctx_tok, d)], axis=0)
    cvec = jnp.concatenate([c, c_ctx[None, :], jnp.zeros((MOD_ROWS - bsz - 1, d), F32)], axis=0)
    mod = _ada_call(cvec, w_ada, b_ada)
    mod3 = mod.reshape(DEPTH * MOD_ROWS * 6, 1, D_MODEL)

    cos_t, sin_t = _rope_tables(geom)
    perm = _qk_col_perm()
    hq = DA_HEADS * 2 * DA_HEAD_DIM
    e_mat = (jnp.arange(D_INNER)[None, :] // SSD_HEAD_DIM == jnp.arange(SSD_HEADS)[:, None]).astype(BF16)
    e_mat = jnp.concatenate([e_mat, e_mat], axis=0)

    w_gate, w_up, w_down = ffn_w_gate.astype(BF16), ffn_w_up.astype(BF16), ffn_w_down.astype(BF16)
    w_ssd_out, w_attn_out = ssd_w_out.astype(BF16), da_w_o.astype(BF16)

    out = None
    for i in range(DEPTH):
        j = i // 2
        last = i == DEPTH - 1
        n1 = norm1_w[i].reshape(1, d)
        if i % 2 == 0:
            w_in = ssd_w_in[j]
            w_zx = w_in[:, :D_ZX].astype(BF16)
            w_dt = w_in[:, D_ZX:].reshape(d, 2, SSD_HEADS).transpose(1, 0, 2)
            w_dt = jnp.pad(w_dt, ((0, 0), (0, 0), (0, LANES - SSD_HEADS))).astype(BF16)
            dt_b = jnp.pad(ssd_dt_bias[j], ((0, 0), (0, LANES - SSD_HEADS))).reshape(2, 1, LANES)
            z, xbc, packed, tr = _ssd_in_call(geom, i, x_all, mod3, n1, w_zx, w_dt, dt_b,
                                              ssd_a_log[j].reshape(2, 1, SSD_HEADS), ssd_conv_w[j],
                                              ssd_conv_b[j].reshape(1, D_CONV_CH))
            y_f, y_b = _ssd_scan_call(geom, xbc, packed, tr, e_mat)
            d_skip_e = jnp.repeat(ssd_d[j], SSD_HEAD_DIM).reshape(1, D_INNER)
            x_all = _ssd_out_call(geom, i, y_f, y_b, xbc, z, d_skip_e, ssd_norm_w[j].reshape(1, D_INNER),
                                  w_ssd_out, x_all, mod3)
        else:
            lambda_init = 0.8 - 0.6 * math.exp(-0.3 * i)
            w_qkv = da_w_qkv[j]
            w_qkv = jnp.concatenate([w_qkv[:, :hq][:, perm], w_qkv[:, hq:2 * hq][:, perm],
                                     w_qkv[:, 2 * hq:]], axis=1).astype(BF16)
            q1, q2, k, v = _qkv_call(geom, i, x_all, mod3, n1, w_qkv, cos_t, sin_t)
            lam_vec = jnp.concatenate([da_lq1[j], da_lk1[j], da_lq2[j], da_lk2[j]]).reshape(1, 4 * DA_HEAD_DIM)
            subw = da_subln_w[j].reshape(1, LANES)
            o_rows = geom.n_lat if last else geom.rows
            o = _flash_call(geom, q1, q2, k, v, lam_vec, subw, lambda_init, None, False, o_rows)
            if not last:
                o = _flash_call(geom, q1, q2, k, v, lam_vec, subw, lambda_init, o, True, o_rows)
            x_all = _attn_out_call(geom, i, o, w_attn_out, x_all, mod3, o_rows)
        res = _ffn_call(geom, i, x_all, mod3, norm2_w[i].reshape(1, d), w_gate, w_up, w_down,
                        final_norm_w.reshape(1, d), final_norm=last)
        if last:
            out = res
        else:
            x_all = res
    return out.reshape(bsz, seq, d)
```

```python
import functools
import math

import jax
import jax.numpy as jnp
from jax import lax
from jax.experimental import pallas as pl
from jax.experimental.pallas import tpu as pltpu

F32 = jnp.float32
BF16 = jnp.bfloat16

D_MODEL = 1024
DEPTH = 4
GRID_W = 64
EPS = 1e-6

D_INNER = 2 * D_MODEL
SSD_HEAD_DIM = 64
SSD_HEADS = D_INNER // SSD_HEAD_DIM
SSD_GROUPS = 4
HEADS_PER_GROUP = SSD_HEADS // SSD_GROUPS
D_STATE = 128
CONV_W = 5
CHUNK = 128
D_BC = SSD_GROUPS * D_STATE
D_CONV_CH = D_INNER + 2 * D_BC
D_ZX = D_INNER + D_CONV_CH

DA_HEADS = D_MODEL // 128
DA_HEAD_DIM = 64
ROPE_THETA = 10000.0
ROPE_PAIRS = DA_HEAD_DIM // 4

D_FF = -(-8 * D_MODEL // (3 * 256)) * 256

LANES = 128
LOG2E = math.log2(math.e)
MOD_ROWS = 16
VMEM_LIMIT = 56 * 1024 * 1024


def _cparams(sem):
    return pltpu.CompilerParams(dimension_semantics=sem, vmem_limit_bytes=VMEM_LIMIT)


def _dot(a, b):
    return jnp.dot(a, b, preferred_element_type=F32)


def _dot_nt(a, b):
    return lax.dot_general(a, b, (((1,), (1,)), ((), ())), preferred_element_type=F32)


def _sigmoid(x):
    return 1.0 / (1.0 + jnp.exp(-x))


def _rms(xf, w):
    return xf * lax.rsqrt(jnp.mean(xf * xf, axis=-1, keepdims=True) + EPS) * w


def _resident(shape, index_map):
    return pl.BlockSpec(shape, index_map, pipeline_mode=pl.Buffered(1))


def _ada_kernel(c_ref, w_ref, b_ref, o_ref):
    cv = c_ref[...]
    s = (cv * _sigmoid(cv)).astype(BF16)
    o_ref[...] = _dot(s, w_ref[...].astype(BF16)) + b_ref[...]


def _ada_call(cvec, w_ada, b_ada):
    depth, d, n = w_ada.shape
    tn = 2048
    return pl.pallas_call(
        _ada_kernel,
        grid=(depth, n // tn),
        in_specs=[
            pl.BlockSpec((MOD_ROWS, d), lambda l, j: (0, 0)),
            pl.BlockSpec((None, d, tn), lambda l, j: (l, 0, j)),
            pl.BlockSpec((None, 1, tn), lambda l, j: (l, 0, j)),
        ],
        out_specs=pl.BlockSpec((None, MOD_ROWS, tn), lambda l, j: (l, 0, j)),
        out_shape=jax.ShapeDtypeStruct((depth, MOD_ROWS, n), F32),
        compiler_params=_cparams(("arbitrary", "arbitrary")),
        name="adaln",
    )(cvec, w_ada, b_ada.reshape(depth, 1, n))


class _Geom:
    def __init__(self, bsz, seq, ctx):
        self.bsz, self.seq, self.ctx = bsz, seq, ctx
        self.n_lat = bsz * seq
        self.n_ctx = bsz * ctx
        self.rows = self.n_lat + self.n_ctx
        tm = 1024
        while seq % tm or self.n_ctx % tm:
            tm //= 2
        self.tm = tm

    def mod_spec(self, layer, k, tm):
        n_lat_tiles = self.n_lat // tm
        per_batch = self.seq // tm
        bsz = self.bsz

        def imap(i, *_):
            row = jnp.where(i < n_lat_tiles, i // per_batch, bsz)
            return ((layer * MOD_ROWS + row) * 6 + k, 0, 0)

        return pl.BlockSpec((None, 1, D_MODEL), imap)


def _softplus(x):
    return jnp.maximum(x, 0.0) + jnp.log1p(jnp.exp(-jnp.abs(x)))


PROJ_TILE = 512


def _split3(v):
    p1 = v.astype(BF16)
    r1 = v - p1.astype(F32)
    p2 = r1.astype(BF16)
    p3 = (r1 - p2.astype(F32)).astype(BF16)
    return p1, p2, p3


def _scan_mask(backward):
    li = lax.broadcasted_iota(jnp.int32, (CHUNK, CHUNK), 0)
    si = lax.broadcasted_iota(jnp.int32, (CHUNK, CHUNK), 1)
    return si >= li if backward else si <= li


def _scan_terms(backward, dt, alog):
    nh = SSD_HEADS
    a = -jnp.exp(alog)
    adt = dt * (a * LOG2E)
    mask = _scan_mask(backward)
    cum3 = _dot(mask.astype(F32).astype(BF16), jnp.concatenate(_split3(adt), axis=1))
    cum = cum3[:, :nh] + cum3[:, nh:2 * nh] + cum3[:, 2 * nh:3 * nh]
    tot = jnp.sum(adt, axis=0, keepdims=True)
    pad = jnp.zeros((CHUNK, CHUNK - nh), F32)
    tr = jnp.concatenate([cum - jnp.log2(dt), pad], axis=1).T[:nh]
    packed = jnp.concatenate([cum, dt * jnp.exp2(tot - cum), jnp.exp2(cum),
                              jnp.broadcast_to(jnp.exp2(tot), (CHUNK, nh))], axis=1)
    return packed, tr


CONV_ROWS = 256
HALO = 16


def _ssd_in_kernel(xp_ref, x_ref, xn_ref, nw_ref, sh_ref, sc_ref, w_ref, wdt_ref, dtb_ref, alog_ref,
                   cw_ref, cb_ref, z_ref, xbc_ref, pk_ref, tr_ref, *, n_lat, seq, ctx):
    def norm_mod(v):
        return (_rms(v, nw_ref[...]) * (1.0 + sc_ref[...]) + sh_ref[...]).astype(BF16)

    ab = norm_mod(x_ref[...])
    tm = ab.shape[0]
    a_ext = jnp.concatenate([norm_mod(xp_ref[...]), ab, norm_mod(xn_ref[...])], axis=0)
    for d in range(2):
        dt = _softplus(_dot(ab, wdt_ref[d]) + dtb_ref[d])
        for c in range(tm // CHUNK):
            packed, tr = _scan_terms(d == 1, dt[c * CHUNK:(c + 1) * CHUNK, :SSD_HEADS], alog_ref[d])
            pk_ref[d, c * CHUNK:(c + 1) * CHUNK, :] = packed
            tr_ref[d, c * SSD_HEADS:(c + 1) * SSD_HEADS, :] = tr
    for c in range(0, D_INNER, PROJ_TILE):
        z_ref[:, c:c + PROJ_TILE] = _dot(ab, w_ref[:, c:c + PROJ_TILE]).astype(BF16)

    row0 = pl.program_id(0) * tm
    keep = []
    for s in range(tm // CONV_ROWS):
        g0 = row0 + s * CONV_ROWS
        g1 = g0 + CONV_ROWS
        starts = jnp.where(g0 < n_lat, g0 % seq == 0, (g0 - n_lat) % ctx == 0)
        ends = jnp.where(g1 <= n_lat, g1 % seq == 0, (g1 - n_lat) % ctx == 0)
        keep.append((jnp.where(starts, 0.0, 1.0), jnp.where(ends, 0.0, 1.0)))
    n_ext = CONV_ROWS + 16
    mid = CONV_W // 2
    for c in range(0, D_CONV_CH, PROJ_TILE):
        acc = _dot(a_ext, w_ref[:, D_INNER + c:D_INNER + c + PROJ_TILE])
        cw = cw_ref[:, c:c + PROJ_TILE]
        cbias = cb_ref[:, c:c + PROJ_TILE]
        for s in range(tm // CONV_ROWS):
            base = HALO + s * CONV_ROWS
            ext = jnp.concatenate([acc[base - 8:base] * keep[s][0], acc[base:base + CONV_ROWS],
                                   acc[base + CONV_ROWS:base + CONV_ROWS + 8] * keep[s][1]], axis=0)
            y = cbias + cw[mid:mid + 1, :] * ext[8:8 + CONV_ROWS]
            for k in range(CONV_W):
                if k != mid:
                    rolled = pltpu.roll(ext, (mid - k) % n_ext, 0)
                    y = y + cw[k:k + 1, :] * rolled[8:8 + CONV_ROWS]
            xbc_ref[s * CONV_ROWS:(s + 1) * CONV_ROWS, c:c + PROJ_TILE] = (y * _sigmoid(y)).astype(BF16)


def _ssd_in_call(geom, layer, x_all, mod3, norm_w, w_zx, w_dt, dt_bias, a_log, conv_w, conv_b):
    tm = min(geom.tm, 512)
    rows = geom.rows
    tr_rows = tm // CHUNK * SSD_HEADS
    hb = tm // HALO
    n_halo = rows // HALO
    kern = functools.partial(_ssd_in_kernel, n_lat=geom.n_lat, seq=geom.seq, ctx=geom.ctx)
    return pl.pallas_call(
        kern,
        grid=(rows // tm,),
        in_specs=[
            pl.BlockSpec((HALO, D_MODEL), lambda i: (jnp.maximum(i * hb - 1, 0), 0)),
            pl.BlockSpec((tm, D_MODEL), lambda i: (i, 0)),
            pl.BlockSpec((HALO, D_MODEL), lambda i: (jnp.minimum((i + 1) * hb, n_halo - 1), 0)),
            pl.BlockSpec((1, D_MODEL), lambda i: (0, 0)),
            geom.mod_spec(layer, 0, tm),
            geom.mod_spec(layer, 1, tm),
            _resident((D_MODEL, D_ZX), lambda i: (0, 0)),
            pl.BlockSpec((2, D_MODEL, LANES), lambda i: (0, 0, 0)),
            pl.BlockSpec((2, 1, LANES), lambda i: (0, 0, 0)),
            pl.BlockSpec((2, 1, SSD_HEADS), lambda i: (0, 0, 0)),
            pl.BlockSpec((CONV_W, D_CONV_CH), lambda i: (0, 0)),
            pl.BlockSpec((1, D_CONV_CH), lambda i: (0, 0)),
        ],
        out_specs=[
            pl.BlockSpec((tm, D_INNER), lambda i: (i, 0)),
            pl.BlockSpec((tm, D_CONV_CH), lambda i: (i, 0)),
            pl.BlockSpec((2, tm, LANES), lambda i: (0, i, 0)),
            pl.BlockSpec((2, tr_rows, LANES), lambda i: (0, i, 0)),
        ],
        out_shape=[
            jax.ShapeDtypeStruct((rows, D_INNER), BF16),
            jax.ShapeDtypeStruct((rows, D_CONV_CH), BF16),
            jax.ShapeDtypeStruct((2, rows, LANES), F32),
            jax.ShapeDtypeStruct((2, rows // CHUNK * SSD_HEADS, LANES), F32),
        ],
        compiler_params=_cparams(("arbitrary",)),
        name="ssd_in_proj_conv",
    )(x_all, x_all, x_all, norm_w, mod3, mod3, w_zx, w_dt, dt_bias, a_log, conv_w, conv_b)


def _expand_heads(v, e2_ref):
    hi = v.astype(BF16)
    lo = (v - hi.astype(F32)).astype(BF16)
    return _dot(jnp.concatenate([hi, lo], axis=1), e2_ref[...])


def _scan_prep(backward, pk_ref, tr_ref, e_ref):
    nh = SSD_HEADS
    pk = pk_ref[...]
    cum = pk[:, :nh]
    ex = _expand_heads(jnp.concatenate([pk[:, nh:2 * nh], pk[:, 2 * nh:3 * nh], pk[:8, 3 * nh:]], axis=0),
                       e_ref)
    return (_scan_mask(backward), cum, tr_ref[...], ex[:CHUNK], ex[CHUNK:2 * CHUNK],
            ex[2 * CHUNK:2 * CHUNK + 1])


def _scan_group(g, prep, xs_ref, b_ref, c_ref, y_ref, h_ref):
    mask, cum, tr, wst_e, dout_e, cdec_e = prep
    lane = lax.broadcasted_iota(jnp.int32, (CHUNK, 2 * SSD_HEAD_DIM), 1)
    gw = HEADS_PER_GROUP * SSD_HEAD_DIM
    bg = b_ref[:, g * D_STATE:(g + 1) * D_STATE]
    cg = c_ref[:, g * D_STATE:(g + 1) * D_STATE]
    cb = _dot_nt(cg, bg)
    h_old = h_ref[:, g * gw:(g + 1) * gw]
    y_off = _dot(cg, h_old.astype(BF16)) * dout_e[:, g * gw:(g + 1) * gw]
    xg = xs_ref[:, g * gw:(g + 1) * gw]
    for q in range(HEADS_PER_GROUP // 2):
        ws = []
        for hh in range(2):
            h = g * HEADS_PER_GROUP + 2 * q + hh
            diff = cum[:, h:h + 1] - tr[h:h + 1, :]
            lmat = jnp.exp2(jnp.where(mask, diff, -jnp.inf))
            ws.append((cb * lmat).astype(BF16))
        xp = xg[:, q * 2 * SSD_HEAD_DIM:(q + 1) * 2 * SSD_HEAD_DIM]
        zero = jnp.zeros_like(xp)
        rhs = jnp.concatenate([jnp.where(lane < SSD_HEAD_DIM, xp, zero),
                               jnp.where(lane >= SSD_HEAD_DIM, xp, zero)], axis=0)
        yd = _dot(jnp.concatenate(ws, axis=1), rhs)
        c0 = g * gw + q * 2 * SSD_HEAD_DIM
        y_ref[:, c0:c0 + 2 * SSD_HEAD_DIM] = (
            yd + y_off[:, q * 2 * SSD_HEAD_DIM:(q + 1) * 2 * SSD_HEAD_DIM]).astype(y_ref.dtype)
    xw = (xg.astype(F32) * wst_e[:, g * gw:(g + 1) * gw]).astype(BF16)
    bgt = bg.astype(F32).T.astype(BF16)
    h_ref[:, g * gw:(g + 1) * gw] = h_old * cdec_e[:, g * gw:(g + 1) * gw] + _dot(bgt, xw)


def _ssd_scan_kernel(xf_ref, bf_ref, cf_ref, pkf_ref, trf_ref, xb_ref, bb_ref, cb_ref, pkb_ref, trb_ref,
                     e_ref, yf_ref, yb_ref, h_ref):
    @pl.when(pl.program_id(1) == 0)
    def _():
        h_ref[...] = jnp.zeros_like(h_ref)

    prep_f = _scan_prep(False, pkf_ref, trf_ref, e_ref)
    prep_b = _scan_prep(True, pkb_ref, trb_ref, e_ref)
    for g in range(SSD_GROUPS):
        _scan_group(g, prep_f, xf_ref, bf_ref, cf_ref, yf_ref, h_ref.at[0])
        _scan_group(g, prep_b, xb_ref, bb_ref, cb_ref, yb_ref, h_ref.at[1])


def _ssd_scan_call(geom, xbc, packed, tr, e_mat):
    rows = geom.rows
    nc_lat = geom.seq // CHUNK
    nc_ctx = geom.ctx // CHUNK
    lat_blocks = geom.n_lat // CHUNK

    def chunk_block(d, b, s):
        in_ctx = s < nc_ctx
        sl = s - nc_ctx
        c_ctx = nc_ctx - 1 - s if d else s
        c_lat = nc_lat - 1 - sl if d else sl
        return jnp.where(in_ctx, lat_blocks + b * nc_ctx + c_ctx, b * nc_lat + c_lat)

    xcols = D_INNER // D_BC

    def dir_specs(d):
        return [
            pl.BlockSpec((CHUNK, D_INNER), lambda b, s: (chunk_block(d, b, s), 0)),
            pl.BlockSpec((CHUNK, D_BC), lambda b, s: (chunk_block(d, b, s), xcols)),
            pl.BlockSpec((CHUNK, D_BC), lambda b, s: (chunk_block(d, b, s), xcols + 1)),
            pl.BlockSpec((None, CHUNK, LANES), lambda b, s: (d, chunk_block(d, b, s), 0)),
            pl.BlockSpec((None, SSD_HEADS, LANES), lambda b, s: (d, chunk_block(d, b, s), 0)),
        ]

    y_shape = jax.ShapeDtypeStruct((rows, D_INNER), BF16)
    return pl.pallas_call(
        _ssd_scan_kernel,
        grid=(geom.bsz, nc_ctx + nc_lat),
        in_specs=dir_specs(0) + dir_specs(1) + [
            pl.BlockSpec((2 * SSD_HEADS, D_INNER), lambda b, s: (0, 0)),
        ],
        out_specs=[pl.BlockSpec((CHUNK, D_INNER), lambda b, s: (chunk_block(0, b, s), 0)),
                   pl.BlockSpec((CHUNK, D_INNER), lambda b, s: (chunk_block(1, b, s), 0))],
        out_shape=[y_shape, y_shape],
        scratch_shapes=[pltpu.VMEM((2, D_STATE, D_INNER), F32)],
        compiler_params=_cparams(("arbitrary", "arbitrary")),
        name="ssd_scan",
    )(xbc, xbc, xbc, packed, tr, xbc, xbc, xbc, packed, tr, e_mat)


def _ssd_out_kernel(yf_ref, yb_ref, xs_ref, z_ref, dsk_ref, gw_ref, w_ref, x_ref, gate_ref, o_ref):
    y = (yf_ref[...].astype(F32) + yb_ref[...].astype(F32)
         + xs_ref[...].astype(F32) * dsk_ref[...])
    z = z_ref[...].astype(F32)
    yn = _rms(y * (z * _sigmoid(z)), gw_ref[...]).astype(BF16)
    o_ref[...] = x_ref[...] + gate_ref[...] * _dot(yn, w_ref[...])


def _ssd_out_call(geom, layer, y_f, y_b, xbc, z, d_skip_e, gnorm_w, w_out, x_all, mod3):
    tm = min(geom.tm, 512)
    rows = geom.rows
    return pl.pallas_call(
        _ssd_out_kernel,
        grid=(rows // tm,),
        in_specs=[
            pl.BlockSpec((tm, D_INNER), lambda i: (i, 0)),
            pl.BlockSpec((tm, D_INNER), lambda i: (i, 0)),
            pl.BlockSpec((tm, D_INNER), lambda i: (i, 0)),
            pl.BlockSpec((tm, D_INNER), lambda i: (i, 0)),
            pl.BlockSpec((1, D_INNER), lambda i: (0, 0)),
            pl.BlockSpec((1, D_INNER), lambda i: (0, 0)),
            _resident((None, D_INNER, D_MODEL), lambda i: (layer // 2, 0, 0)),
            pl.BlockSpec((tm, D_MODEL), lambda i: (i, 0)),
            geom.mod_spec(layer, 2, tm),
        ],
        out_specs=pl.BlockSpec((tm, D_MODEL), lambda i: (i, 0)),
        out_shape=jax.ShapeDtypeStruct((rows, D_MODEL), F32),
        compiler_params=_cparams(("arbitrary",)),
        name="ssd_out_proj",
    )(y_f, y_b, xbc, z, d_skip_e, gnorm_w, w_out, x_all, mod3)


def _rope_blocks(acc, cos, sin):
    outs = []
    for c in range(acc.shape[1] // LANES):
        xb = acc[:, c * LANES:(c + 1) * LANES]
        outs.append(xb * cos + pltpu.roll(xb, LANES // 2, 1) * sin)
    return outs


def _qkv_kernel(x_ref, nw_ref, sh_ref, sc_ref, w_ref, cos_ref, sin_ref, q1_ref, q2_ref, k_ref, v_ref):
    a = _rms(x_ref[...], nw_ref[...]) * (1.0 + sc_ref[...]) + sh_ref[...]
    ab = a.astype(BF16)
    cos, sin = cos_ref[...], sin_ref[...]
    scale = DA_HEAD_DIM ** -0.5 * LOG2E
    lane = lax.broadcasted_iota(jnp.int32, (ab.shape[0], LANES), 1)
    is_map1 = (lane % (LANES // 2)) < (LANES // 4)
    for c0 in range(0, D_MODEL, PROJ_TILE):
        acc = _dot(ab, w_ref[:, c0:c0 + PROJ_TILE])
        for c, blk in enumerate(_rope_blocks(acc, cos, sin)):
            blk = blk * scale
            col = c0 + c * LANES
            q1_ref[:, col:col + LANES] = jnp.where(is_map1, blk, 0.0).astype(BF16)
            q2_ref[:, col:col + LANES] = jnp.where(is_map1, 0.0, blk).astype(BF16)
    for c0 in range(0, D_MODEL, PROJ_TILE):
        acc = _dot(ab, w_ref[:, D_MODEL + c0:D_MODEL + c0 + PROJ_TILE])
        for c, blk in enumerate(_rope_blocks(acc, cos, sin)):
            col = c0 + c * LANES
            k_ref[:, col:col + LANES] = blk.astype(BF16)
    for c0 in range(0, D_MODEL, PROJ_TILE):
        v_ref[:, c0:c0 + PROJ_TILE] = _dot(
            ab, w_ref[:, 2 * D_MODEL + c0:2 * D_MODEL + c0 + PROJ_TILE]).astype(BF16)


def _qkv_call(geom, layer, x_all, mod3, norm_w, w_qkv, cos_t, sin_t):
    tm = min(geom.tm, 512)
    rows = geom.rows
    row_out = pl.BlockSpec((tm, D_MODEL), lambda i: (i, 0))
    return pl.pallas_call(
        _qkv_kernel,
        grid=(rows // tm,),
        in_specs=[
            pl.BlockSpec((tm, D_MODEL), lambda i: (i, 0)),
            pl.BlockSpec((1, D_MODEL), lambda i: (0, 0)),
            geom.mod_spec(layer, 0, tm),
            geom.mod_spec(layer, 1, tm),
            _resident((D_MODEL, 3 * D_MODEL), lambda i: (0, 0)),
            pl.BlockSpec((tm, LANES), lambda i: (i, 0)),
            pl.BlockSpec((tm, LANES), lambda i: (i, 0)),
        ],
        out_specs=[row_out, row_out, row_out, row_out],
        out_shape=[jax.ShapeDtypeStruct((rows, D_MODEL), BF16)] * 4,
        compiler_params=_cparams(("arbitrary",)),
        name="attn_qkv_rope",
    )(x_all, norm_w, mod3, mod3, w_qkv, cos_t, sin_t)


FLASH_ROWS = 64


def _flash_kernel(lam_ref, subw_ref, q1_ref, q2_ref, *refs, seg_lens, tq, aliased, lambda_init):
    n_seg = len(seg_lens)
    kv_refs = refs[:2 * n_seg]
    o_ref = refs[2 * n_seg + (1 if aliased else 0)]
    scratch = refs[2 * n_seg + (2 if aliased else 1):]

    lam_v = lam_ref[...]
    hd = DA_HEAD_DIM
    lam = (jnp.exp(jnp.sum(lam_v[:, 0:hd] * lam_v[:, hd:2 * hd], axis=1, keepdims=True))
           - jnp.exp(jnp.sum(lam_v[:, 2 * hd:3 * hd] * lam_v[:, 3 * hd:4 * hd], axis=1, keepdims=True))
           + lambda_init)

    def q_tile(qrows, st):
        s_scr, p_scr = (r.at[st] for r in scratch)
        qs = (q1_ref[qrows, :], q2_ref[qrows, :])
        offs = [sum(seg_lens[:j]) for j in range(n_seg)]
        n_keys = sum(seg_lens)
        for t in range(2):
            for j in range(n_seg):
                s_scr[t, :, offs[j]:offs[j] + seg_lens[j]] = _dot_nt(qs[t], kv_refs[2 * j][...])
        accs = []
        for t in range(2):
            for r in range(0, tq, FLASH_ROWS):
                rows = pl.ds(r, FLASH_ROWS)
                tiles = [s_scr[t, rows, c:c + LANES] for c in range(0, n_keys, LANES)]
                mx = tiles[0]
                for tl in tiles[1:]:
                    mx = jnp.maximum(mx, tl)
                m = jnp.max(mx, axis=1, keepdims=True)
                for ci, tl in enumerate(tiles):
                    p_scr[t, rows, ci * LANES:(ci + 1) * LANES] = jnp.exp2(tl - m).astype(BF16)
            acc = None
            for j in range(n_seg):
                v_blk = kv_refs[2 * j + 1][...]
                vaug = jnp.concatenate([v_blk, jnp.ones_like(v_blk)], axis=1)
                part = _dot(p_scr[t, :, offs[j]:offs[j] + seg_lens[j]], vaug)
                acc = part if acc is None else acc + part
            accs.append(acc)
        a1, a2 = accs
        o = a1[:, :LANES] / a1[:, LANES:] - lam * (a2[:, :LANES] / a2[:, LANES:])
        o = _rms(o, subw_ref[...]) * (1.0 - lambda_init)
        o_ref[qrows, :] = o.astype(BF16)

    n_sub = q1_ref.shape[0] // tq
    if n_sub == 1:
        q_tile(pl.ds(0, tq), 0)
    else:
        def body(qi, carry):
            q_tile(pl.ds(pl.multiple_of(2 * qi * tq, tq), tq), 0)
            q_tile(pl.ds(pl.multiple_of((2 * qi + 1) * tq, tq), tq), 1)
            return carry
        lax.fori_loop(0, n_sub // 2, body, 0)


Q_SUBTILES = 4


def _flash_call(geom, q1, q2, k, v, lam_vec, subln_w, lambda_init, o_prev, ctx_queries, out_rows):
    bsz, seq, ctx = geom.bsz, geom.seq, geom.ctx
    if ctx_queries:
        tq = min(256, ctx)
        qb = tq
        nq = ctx // qb
        q_base = geom.n_lat // qb
        seg_lens = (ctx,)
    else:
        tq = min(256, seq)
        qb = min(Q_SUBTILES * tq, seq)
        nq = seq // qb
        q_base = 0
        seg_lens = (seq, ctx)
    ctx_base = geom.n_lat // ctx
    aliased = o_prev is not None
    n_sets = 1 if qb == tq else 2
    assert (qb // tq) % n_sets == 0

    def q_map(b, h, i):
        return (q_base + b * nq + i, h)

    kv_lat = pl.BlockSpec((seq, LANES), lambda b, h, i: (b, h))
    kv_ctx = pl.BlockSpec((ctx, LANES), lambda b, h, i: (ctx_base + b, h))
    kv_specs = [kv_ctx, kv_ctx] if ctx_queries else [kv_lat, kv_lat, kv_ctx, kv_ctx]
    kv_args = [k, v] if ctx_queries else [k, v, k, v]
    kern = functools.partial(_flash_kernel, seg_lens=seg_lens, tq=tq, aliased=aliased,
                             lambda_init=lambda_init)
    n_in = 4 + len(kv_args)
    return pl.pallas_call(
        kern,
        grid=(bsz, DA_HEADS, nq),
        in_specs=[
            pl.BlockSpec((1, 4 * DA_HEAD_DIM), lambda b, h, i: (0, 0)),
            pl.BlockSpec((1, LANES), lambda b, h, i: (0, 0)),
            pl.BlockSpec((qb, LANES), q_map),
            pl.BlockSpec((qb, LANES), q_map),
            *kv_specs,
        ] + ([pl.BlockSpec(memory_space=pl.ANY)] if aliased else []),
        out_specs=pl.BlockSpec((qb, LANES), q_map),
        out_shape=jax.ShapeDtypeStruct((out_rows, D_MODEL), BF16),
        input_output_aliases={n_in: 0} if aliased else {},
        scratch_shapes=[pltpu.VMEM((n_sets, 2, tq, sum(seg_lens)), F32),
                        pltpu.VMEM((n_sets, 2, tq, sum(seg_lens)), BF16)],
        compiler_params=_cparams(("arbitrary", "arbitrary", "arbitrary")),
        name="flash_ctx" if ctx_queries else "flash_lat",
    )(lam_vec, subln_w, q1, q2, *kv_args, *([o_prev] if aliased else []))


def _attn_out_kernel(o_ref, w_ref, x_ref, gate_ref, out_ref):
    out_ref[...] = x_ref[...] + gate_ref[...] * _dot(o_ref[...], w_ref[...])


def _attn_out_call(geom, layer, o, w_o, x_all, mod3, rows):
    tm = geom.tm
    return pl.pallas_call(
        _attn_out_kernel,
        grid=(rows // tm,),
        in_specs=[
            pl.BlockSpec((tm, D_MODEL), lambda i: (i, 0)),
            _resident((None, D_MODEL, D_MODEL), lambda i: (layer // 2, 0, 0)),
            pl.BlockSpec((tm, D_MODEL), lambda i: (i, 0)),
            geom.mod_spec(layer, 2, tm),
        ],
        out_specs=pl.BlockSpec((tm, D_MODEL), lambda i: (i, 0)),
        out_shape=jax.ShapeDtypeStruct((rows, D_MODEL), F32),
        compiler_params=_cparams(("arbitrary",)),
        name="attn_out_proj",
    )(o, w_o, x_all, mod3)


FFN_TILE = 256


def _ffn_kernel(x_ref, nw_ref, sh_ref, sc_ref, gate_ref, wg_ref, wu_ref, wd_ref, fw_ref,
                o_ref, h_scr, *, final_norm):
    x = x_ref[...]
    a = (_rms(x, nw_ref[...]) * (1.0 + sc_ref[...]) + sh_ref[...]).astype(BF16)
    for f in range(0, D_FF, FFN_TILE):
        g = _dot(a, wg_ref[:, f:f + FFN_TILE])
        u = _dot(a, wu_ref[:, f:f + FFN_TILE])
        h_scr[:, f:f + FFN_TILE] = (g * _sigmoid(g) * u).astype(BF16)
    out = x + gate_ref[...] * _dot(h_scr[...], wd_ref[...])
    if final_norm:
        out = _rms(out, fw_ref[...])
    o_ref[...] = out


def _ffn_call(geom, layer, x_all, mod3, norm_w, w_gate, w_up, w_down, final_w, final_norm):
    tm = min(geom.tm, 512)
    rows = geom.n_lat if final_norm else geom.rows
    kern = functools.partial(_ffn_kernel, final_norm=final_norm)
    return pl.pallas_call(
        kern,
        grid=(rows // tm,),
        in_specs=[
            pl.BlockSpec((tm, D_MODEL), lambda i: (i, 0)),
            pl.BlockSpec((1, D_MODEL), lambda i: (0, 0)),
            geom.mod_spec(layer, 3, tm),
            geom.mod_spec(layer, 4, tm),
            geom.mod_spec(layer, 5, tm),
            _resident((None, D_MODEL, D_FF), lambda i: (layer, 0, 0)),
            _resident((None, D_MODEL, D_FF), lambda i: (layer, 0, 0)),
            _resident((None, D_FF, D_MODEL), lambda i: (layer, 0, 0)),
            pl.BlockSpec((1, D_MODEL), lambda i: (0, 0)),
        ],
        out_specs=pl.BlockSpec((tm, D_MODEL), lambda i: (i, 0)),
        out_shape=jax.ShapeDtypeStruct((rows, D_MODEL), F32),
        scratch_shapes=[pltpu.VMEM((tm, D_FF), BF16)],
        compiler_params=_cparams(("arbitrary",)),
        name="ffn_swiglu",
    )(x_all, norm_w, mod3, mod3, mod3, w_gate, w_up, w_down, final_w)


def _rope_tables(geom):
    seq = geom.seq
    t = jnp.arange(seq)
    row = (t // GRID_W).astype(F32)
    col = (t % GRID_W).astype(F32)
    freqs = 1.0 / (ROPE_THETA ** (jnp.arange(ROPE_PAIRS, dtype=F32) / ROPE_PAIRS))
    ang = jnp.concatenate([row[:, None] * freqs, col[:, None] * freqs], axis=-1)
    cos, sin = jnp.cos(ang), jnp.sin(ang)
    cos_l = jnp.tile(cos, (geom.bsz, 4))
    sin_l = jnp.tile(jnp.concatenate([-sin, -sin, sin, sin], axis=-1), (geom.bsz, 1))
    cos_t = jnp.concatenate([cos_l, jnp.ones((geom.n_ctx, LANES), F32)], axis=0)
    sin_t = jnp.concatenate([sin_l, jnp.zeros((geom.n_ctx, LANES), F32)], axis=0)
    return cos_t, sin_t


def _qk_col_perm():
    perm = []
    quarter = DA_HEAD_DIM // 2
    for h in range(DA_HEADS):
        for half in range(2):
            for m in range(2):
                base = h * 2 * DA_HEAD_DIM + m * DA_HEAD_DIM + half * quarter
                perm.extend(range(base, base + quarter))
    return jnp.asarray(perm, dtype=jnp.int32)


def kernel(x, c, ctx, c_ctx, w_ada, b_ada, norm1_w, norm2_w, ssd_w_in, ssd_conv_w, ssd_conv_b,
           ssd_a_log, ssd_dt_bias, ssd_d, ssd_norm_w, ssd_w_out, da_w_qkv, da_w_o, da_lq1, da_lk1,
           da_lq2, da_lk2, da_subln_w, ffn_w_gate, ffn_w_up, ffn_w_down, final_norm_w):
    bsz, seq, d = x.shape
    n_ctx_tok = ctx.shape[1]
    assert d == D_MODEL and bsz + 1 <= MOD_ROWS
    assert seq % CONV_ROWS == 0 and n_ctx_tok % CONV_ROWS == 0 and seq % GRID_W == 0
    geom = _Geom(bsz, seq, n_ctx_tok)

    x_all = jnp.concatenate([x.reshape(bsz * seq, d), ctx.reshape(bsz * n_ctx_tok, d)], axis=0)
    cvec = jnp.concatenate([c, c_ctx[None, :], jnp.zeros((MOD_ROWS - bsz - 1, d), F32)], axis=0)
    mod = _ada_call(cvec, w_ada, b_ada)
    mod3 = mod.reshape(DEPTH * MOD_ROWS * 6, 1, D_MODEL)

    cos_t, sin_t = _rope_tables(geom)
    perm = _qk_col_perm()
    hq = DA_HEADS * 2 * DA_HEAD_DIM
    e_mat = (jnp.arange(D_INNER)[None, :] // SSD_HEAD_DIM == jnp.arange(SSD_HEADS)[:, None]).astype(BF16)
    e_mat = jnp.concatenate([e_mat, e_mat], axis=0)

    w_gate, w_up, w_down = ffn_w_gate.astype(BF16), ffn_w_up.astype(BF16), ffn_w_down.astype(BF16)
    w_ssd_out, w_attn_out = ssd_w_out.astype(BF16), da_w_o.astype(BF16)

    out = None
    for i in range(DEPTH):
        j = i // 2
        last = i == DEPTH - 1
        n1 = norm1_w[i].reshape(1, d)
        if i % 2 == 0:
            w_in = ssd_w_in[j]
            w_zx = w_in[:, :D_ZX].astype(BF16)
            w_dt = w_in[:, D_ZX:].reshape(d, 2, SSD_HEADS).transpose(1, 0, 2)
            w_dt = jnp.pad(w_dt, ((0, 0), (0, 0), (0, LANES - SSD_HEADS))).astype(BF16)
            dt_b = jnp.pad(ssd_dt_bias[j], ((0, 0), (0, LANES - SSD_HEADS))).reshape(2, 1, LANES)
            z, xbc, packed, tr = _ssd_in_call(geom, i, x_all, mod3, n1, w_zx, w_dt, dt_b,
                                              ssd_a_log[j].reshape(2, 1, SSD_HEADS), ssd_conv_w[j],
                                              ssd_conv_b[j].reshape(1, D_CONV_CH))
            y_f, y_b = _ssd_scan_call(geom, xbc, packed, tr, e_mat)
            d_skip_e = jnp.repeat(ssd_d[j], SSD_HEAD_DIM).reshape(1, D_INNER)
            x_all = _ssd_out_call(geom, i, y_f, y_b, xbc, z, d_skip_e, ssd_norm_w[j].reshape(1, D_INNER),
                                  w_ssd_out, x_all, mod3)
        else:
            lambda_init = 0.8 - 0.6 * math.exp(-0.3 * i)
            w_qkv = da_w_qkv[j]
            w_qkv = jnp.concatenate([w_qkv[:, :hq][:, perm], w_qkv[:, hq:2 * hq][:, perm],
                                     w_qkv[:, 2 * hq:]], axis=1).astype(BF16)
            q1, q2, k, v = _qkv_call(geom, i, x_all, mod3, n1, w_qkv, cos_t, sin_t)
            lam_vec = jnp.concatenate([da_lq1[j], da_lk1[j], da_lq2[j], da_lk2[j]]).reshape(1, 4 * DA_HEAD_DIM)
            subw = da_subln_w[j].reshape(1, LANES)
            o_rows = geom.n_lat if last else geom.rows
            o = _flash_call(geom, q1, q2, k, v, lam_vec, subw, lambda_init, None, False, o_rows)
            if not last:
                o = _flash_call(geom, q1, q2, k, v, lam_vec, subw, lambda_init, o, True, o_rows)
            x_all = _attn_out_call(geom, i, o, w_attn_out, x_all, mod3, o_rows)
        res = _ffn_call(geom, i, x_all, mod3, norm2_w[i].reshape(1, d), w_gate, w_up, w_down,
                        final_norm_w.reshape(1, d), final_norm=last)
        if last:
            out = res
        else:
            x_all = res
    return out.reshape(bsz, seq, d)
```
